```python
import jax, jax.numpy as jnp
from jax import lax
import numpy as np

D_MODEL = 1024
BATCH = 32
SEQ = 2048
DEPTH = 2

CONV_CH = 512
CONV_WIDTH = 31
ATTN_HEADS = 8
HEAD_DIM = 64
ATTN_W = ATTN_HEADS * HEAD_DIM
Q_BLOCK = 128
D_FF = 2816
FFN_CONV_WIDTH = 3
LN_EPS = 1e-5
FORGET_BIAS_INIT = 3.0
ALPHA = (2.0 * DEPTH) ** 0.25
BETA = (8.0 * DEPTH) ** -0.25

GLU_OFF = 0
Q_OFF = GLU_OFF + 2 * CONV_CH
K_OFF = Q_OFF + ATTN_W
V_OFF = K_OFF + ATTN_W
F_OFF = V_OFF + ATTN_W
G_OFF = F_OFF + ATTN_HEADS
N_IN = G_OFF + 2 * D_MODEL

kernel_name = 'hybrid_conformer_fox_convffn_deepnorm_adaln'


def layer_norm(x, g=None, b=None):
    xf = x.astype(jnp.float32)
    mu = jnp.mean(xf, axis=-1, keepdims=True)
    xc = xf - mu
    var = jnp.mean(xc * xc, axis=-1, keepdims=True)
    y = (xc * lax.rsqrt(var + LN_EPS)).astype(x.dtype)
    if g is not None:
        y = y * g + b
    return y


def causal_dwconv(x, w, b):
    k_w, ch = w.shape
    y = lax.conv_general_dilated(
        x, w[:, None, :], window_strides=(1,), padding=[(k_w - 1, 0)],
        dimension_numbers=('NWC', 'WIO', 'NWC'), feature_group_count=ch)
    return y + b


def forgetting_attention(q, k, v, log_f):
    seq = q.shape[1]
    scale = HEAD_DIM ** -0.5
    qh = jnp.transpose(q, (0, 2, 1, 3)) * scale
    kh = jnp.transpose(k, (0, 2, 1, 3))
    vh = jnp.transpose(v, (0, 2, 1, 3))
    cum = jnp.transpose(jnp.cumsum(log_f, axis=1), (0, 2, 1))
    neg = jnp.finfo(jnp.float32).min
    outs = []
    for i in range(seq // Q_BLOCK):
        q0 = i * Q_BLOCK
        q1 = q0 + Q_BLOCK
        qb = qh[:, :, q0:q1]
        kb = kh[:, :, :q1]
        vb = vh[:, :, :q1]
        logits = jnp.einsum('bhqd,bhkd->bhqk', qb, kb).astype(jnp.float32)
        logits = logits + cum[:, :, q0:q1, None] - cum[:, :, None, :q1]
        mask = (q0 + jnp.arange(Q_BLOCK))[:, None] >= jnp.arange(q1)[None, :]
        logits = jnp.where(mask[None, None], logits, neg)
        p = jax.nn.softmax(logits, axis=-1).astype(v.dtype)
        outs.append(jnp.einsum('bhqk,bhkd->bhqd', p, vb))
    o = jnp.concatenate(outs, axis=2)
    bsz = q.shape[0]
    return jnp.transpose(o, (0, 2, 1, 3)).reshape(bsz, seq, ATTN_W)


def hybrid_layer(x, c_act, w_ada, b_ada, w_in, b_in, conv_a_w, conv_a_b,
                 ln_conv_g, ln_conv_b, w_conv_proj, w_attn_proj, w_mix_out, b_mix_out,
                 ln1_g, ln1_b, w_ffn_up, ffn_conv_w, ffn_conv_b, w_ffn_down, ln2_g, ln2_b):
    bsz, seq, _ = x.shape
    mod = c_act @ w_ada + b_ada
    shift1, scale1, gate1, shift2, scale2, gate2 = jnp.split(mod[:, None, :], 6, axis=-1)

    u = layer_norm(x) * (1.0 + scale1) + shift1
    z = u @ w_in + b_in

    a = z[..., GLU_OFF:GLU_OFF + CONV_CH] * jax.nn.sigmoid(z[..., GLU_OFF + CONV_CH:Q_OFF])
    a = causal_dwconv(a, conv_a_w, conv_a_b)
    a = jax.nn.silu(layer_norm(a, ln_conv_g, ln_conv_b))
    y_a = a @ w_conv_proj

    q = z[..., Q_OFF:K_OFF].reshape(bsz, seq, ATTN_HEADS, HEAD_DIM)
    k = z[..., K_OFF:V_OFF].reshape(bsz, seq, ATTN_HEADS, HEAD_DIM)
    v = z[..., V_OFF:F_OFF].reshape(bsz, seq, ATTN_HEADS, HEAD_DIM)
    log_f = jax.nn.log_sigmoid(z[..., F_OFF:G_OFF].astype(jnp.float32))
    y_b = forgetting_attention(q, k, v, log_f) @ w_attn_proj

    g_a = jax.nn.sigmoid(z[..., G_OFF:G_OFF + D_MODEL])
    g_b = jax.nn.sigmoid(z[..., G_OFF + D_MODEL:N_IN])
    mix = (g_a * y_a + g_b * y_b) @ w_mix_out + b_mix_out
    x = layer_norm(ALPHA * x + (1.0 + gate1) * mix, ln1_g, ln1_b)

    u = layer_norm(x) * (1.0 + scale2) + shift2
    h = causal_dwconv(u @ w_ffn_up, ffn_conv_w, ffn_conv_b)
    f = jax.nn.gelu(h[..., :D_FF], approximate=False) * h[..., D_FF:]
    ffn = f @ w_ffn_down
    x = layer_norm(ALPHA * x + (1.0 + gate2) * ffn, ln2_g, ln2_b)
    return x


def setup_inputs(seed: int = 0) -> dict:
    key = jax.random.key(seed)
    ks = jax.random.split(key, 24)

    def nrm(k, shape, scale):
        return jax.random.normal(k, shape, jnp.float32) * scale

    L, D = DEPTH, D_MODEL
    w_in = nrm(ks[4], (L, D, N_IN), D ** -0.5)
    w_in = w_in.at[:, :, V_OFF:F_OFF].multiply(BETA)
    b_in = nrm(ks[5], (L, N_IN), 0.02).at[:, F_OFF:G_OFF].add(FORGET_BIAS_INIT)
    return {
        'x': nrm(ks[0], (BATCH, SEQ, D), 1.0),
        'c': nrm(ks[1], (BATCH, D), 1.0),
        'w_ada': nrm(ks[2], (L, D, 6 * D), 0.5 * D ** -0.5),
        'b_ada': nrm(ks[3], (L, 6 * D), 0.02),
        'w_in': w_in,
        'b_in': b_in,
        'conv_a_w': nrm(ks[6], (L, CONV_WIDTH, CONV_CH), CONV_WIDTH ** -0.5),
        'conv_a_b': nrm(ks[7], (L, CONV_CH), 0.02),
        'ln_conv_g': 1.0 + nrm(ks[8], (L, CONV_CH), 0.02),
        'ln_conv_b': nrm(ks[9], (L, CONV_CH), 0.02),
        'w_conv_proj': nrm(ks[10], (L, CONV_CH, D), BETA * CONV_CH ** -0.5),
        'w_attn_proj': nrm(ks[11], (L, ATTN_W, D), BETA * ATTN_W ** -0.5),
        'w_mix_out': nrm(ks[12], (L, D, D), BETA * D ** -0.5),
        'b_mix_out': nrm(ks[13], (L, D), 0.02),
        'ln1_g': 1.0 + nrm(ks[14], (L, D), 0.02),
        'ln1_b': nrm(ks[15], (L, D), 0.02),
        'w_ffn_up': nrm(ks[16], (L, D, 2 * D_FF), BETA * D ** -0.5),
        'ffn_conv_w': nrm(ks[17], (L, FFN_CONV_WIDTH, 2 * D_FF), FFN_CONV_WIDTH ** -0.5),
        'ffn_conv_b': nrm(ks[18], (L, 2 * D_FF), 0.02),
        'w_ffn_down': nrm(ks[19], (L, D_FF, D), BETA * D_FF ** -0.5),
        'ln2_g': 1.0 + nrm(ks[20], (L, D), 0.02),
        'ln2_b': nrm(ks[21], (L, D), 0.02),
    }


def reference(x, c, w_ada, b_ada, w_in, b_in, conv_a_w, conv_a_b, ln_conv_g, ln_conv_b,
              w_conv_proj, w_attn_proj, w_mix_out, b_mix_out, ln1_g, ln1_b,
              w_ffn_up, ffn_conv_w, ffn_conv_b, w_ffn_down, ln2_g, ln2_b):
    c_act = jax.nn.silu(c)
    for l in range(DEPTH):
        x = hybrid_layer(
            x, c_act, w_ada[l], b_ada[l], w_in[l], b_in[l], conv_a_w[l], conv_a_b[l],
            ln_conv_g[l], ln_conv_b[l], w_conv_proj[l], w_attn_proj[l], w_mix_out[l],
            b_mix_out[l], ln1_g[l], ln1_b[l], w_ffn_up[l], ffn_conv_w[l], ffn_conv_b[l],
            w_ffn_down[l], ln2_g[l], ln2_b[l])
    return x
```

```python
import functools
import math

import jax
import jax.numpy as jnp
from jax import lax
from jax.experimental import pallas as pl
from jax.experimental.pallas import tpu as pltpu

CONV_CH = 512
CONV_WIDTH = 31
ATTN_HEADS = 8
HEAD_DIM = 64
ATTN_W = ATTN_HEADS * HEAD_DIM
D_FF = 2816
FFN_CONV_WIDTH = 3
LN_EPS = 1e-5

LANES = 128
SUBLANES = 8
VMEM_LIMIT = 56 * 1024 * 1024

P_GLU = 0
P_Q = P_GLU + 2 * CONV_CH
P_K = P_Q + ATTN_W
P_V = P_K + ATTN_W
P_F = P_V + ATTN_W

F32 = jnp.float32
BF16 = jnp.bfloat16
NEG_BIG = -1e30


def _const_spec(shape):
    nd = len(shape)
    return pl.BlockSpec(shape, lambda *_: (0,) * nd, pipeline_mode=pl.Buffered(1))


def _normalize(x):
    mu = jnp.mean(x, axis=-1, keepdims=True)
    xc = x - mu
    var = jnp.mean(xc * xc, axis=-1, keepdims=True)
    return xc * lax.rsqrt(var + LN_EPS)


def _sigmoid(x):
    return jax.nn.sigmoid(x)


def _split3(x):
    hi = x.astype(BF16)
    r1 = x - hi.astype(F32)
    mid = r1.astype(BF16)
    lo = (r1 - mid.astype(F32)).astype(BF16)
    return hi, mid, lo


def _mod_kernel(c_ref, w_ref, b_ref, o_ref):
    c = c_ref[...]
    c_act = c * _sigmoid(c)
    o_ref[0] = jnp.dot(c_act, w_ref[0], preferred_element_type=F32,
                       precision=lax.Precision.HIGHEST) + b_ref[0]


def _modulation(c, w_ada, b_ada):
    depth, d, n = w_ada.shape
    bsz = c.shape[0]
    tn = n // 6
    return pl.pallas_call(
        _mod_kernel,
        grid=(depth, n // tn),
        in_specs=[
            pl.BlockSpec((bsz, d), lambda l, j: (0, 0)),
            pl.BlockSpec((1, d, tn), lambda l, j: (l, 0, j)),
            pl.BlockSpec((1, 1, tn), lambda l, j: (l, 0, j)),
        ],
        out_specs=pl.BlockSpec((1, bsz, tn), lambda l, j: (l, 0, j)),
        out_shape=jax.ShapeDtypeStruct((depth, bsz, n), F32),
        compiler_params=pltpu.CompilerParams(
            dimension_semantics=("arbitrary", "arbitrary"), vmem_limit_bytes=VMEM_LIMIT),
        name="adaln_modulation",
    )(c, w_ada, b_ada.reshape(depth, 1, n))


def _inproj_kernel(x_ref, mod_ref, w_ref, b_ref, wg_ref, bg_ref,
                   a_ref, q_ref, k_ref, v_ref, ga_ref, gb_ref, cum_ref, carry_ref, *, d):
    i = pl.program_id(1)
    tm = x_ref.shape[1]
    shift = mod_ref[0, :, 0:d]
    scale = mod_ref[0, :, d:2 * d]
    u = (_normalize(x_ref[0]) * (1.0 + scale) + shift).astype(BF16)

    def proj(lo, hi):
        return jnp.dot(u, w_ref[:, lo:hi], preferred_element_type=F32) + b_ref[:, lo:hi]

    val = proj(P_GLU, P_GLU + CONV_CH)
    gate = proj(P_GLU + CONV_CH, P_Q)
    a_ref[0] = (val * _sigmoid(gate)).astype(BF16)
    q_ref[0] = proj(P_Q, P_K).astype(BF16)
    k_ref[0] = proj(P_K, P_V).astype(BF16)
    v_ref[0] = proj(P_V, P_F).astype(BF16)
    ga_ref[0] = _sigmoid(jnp.dot(u, wg_ref[:, 0:d], preferred_element_type=F32)
                         + bg_ref[:, 0:d]).astype(BF16)
    gb_ref[0] = _sigmoid(jnp.dot(u, wg_ref[:, d:2 * d], preferred_element_type=F32)
                         + bg_ref[:, d:2 * d]).astype(BF16)

    zf = proj(P_F, P_F + LANES)
    log_f = jnp.minimum(zf, 0.0) - jnp.log1p(jnp.exp(-jnp.abs(zf)))

    @pl.when(i == 0)
    def _():
        carry_ref[...] = jnp.zeros_like(carry_ref)

    row = lax.broadcasted_iota(jnp.int32, (LANES, LANES), 0)
    col = lax.broadcasted_iota(jnp.int32, (LANES, LANES), 1)
    tri = jnp.where(row >= col, 1.0, 0.0).astype(BF16)
    carry = carry_ref[0:1, :]
    for r in range(tm // LANES):
        hi, mid, lo = _split3(log_f[r * LANES:(r + 1) * LANES, :])
        cs = (jnp.dot(tri, hi, preferred_element_type=F32)
              + jnp.dot(tri, mid, preferred_element_type=F32)
              + jnp.dot(tri, lo, preferred_element_type=F32)) + carry
        cum_ref[0, r * LANES:(r + 1) * LANES, :] = cs
        carry = cs[LANES - 1:LANES, :]
    carry_ref[0:1, :] = carry


def _inproj(x, mod, w_main, b_main, w_gate, b_gate, tm):
    bsz, seq, d = x.shape
    n_main = w_main.shape[1]
    grid = (bsz, seq // tm)
    row_spec = lambda w: pl.BlockSpec((1, tm, w), lambda b, i: (b, i, 0))
    out_shape = (
        jax.ShapeDtypeStruct((bsz, seq, CONV_CH), BF16),
        jax.ShapeDtypeStruct((bsz, seq, ATTN_W), BF16),
        jax.ShapeDtypeStruct((bsz, seq, ATTN_W), BF16),
        jax.ShapeDtypeStruct((bsz, seq, ATTN_W), BF16),
        jax.ShapeDtypeStruct((bsz, seq, d), BF16),
        jax.ShapeDtypeStruct((bsz, seq, d), BF16),
        jax.ShapeDtypeStruct((bsz, seq, LANES), F32),
    )
    return pl.pallas_call(
        functools.partial(_inproj_kernel, d=d),
        grid=grid,
        in_specs=[
            row_spec(d),
            pl.BlockSpec((1, 1, mod.shape[-1]), lambda b, i: (b, 0, 0)),
            _const_spec((d, n_main)),
            _const_spec((1, n_main)),
            _const_spec((d, 2 * d)),
            _const_spec((1, 2 * d)),
        ],
        out_specs=(row_spec(CONV_CH), row_spec(ATTN_W), row_spec(ATTN_W), row_spec(ATTN_W),
                   row_spec(d), row_spec(d), row_spec(LANES)),
        out_shape=out_shape,
        scratch_shapes=[pltpu.VMEM((SUBLANES, LANES), F32)],
        compiler_params=pltpu.CompilerParams(
            dimension_semantics=("arbitrary", "arbitrary"), vmem_limit_bytes=VMEM_LIMIT),
        name="inproj",
    )(x, mod, w_main, b_main, w_gate, b_gate)


def _head_cum(cum, head):
    lane = lax.broadcasted_iota(jnp.int32, cum.shape, 1)
    return jnp.sum(jnp.where(lane == head, cum, 0.0), axis=-1, keepdims=True)


def _augment(x, c, hh, is_query):
    rows = x.shape[0]
    lane = lax.broadcasted_iota(jnp.int32, (rows, LANES), 1)
    own = (lane >= hh * HEAD_DIM) & (lane < (hh + 1) * HEAD_DIM)
    base = (1 - hh) * HEAD_DIM
    hi, mid, lo = _split3(c)
    hi, mid, lo = hi.astype(F32), mid.astype(F32), lo.astype(F32)
    if is_query:
        parts = (hi, mid, lo, 1.0, 1.0, 1.0)
    else:
        parts = (1.0, 1.0, 1.0, -hi, -mid, -lo)
    aug = jnp.zeros((rows, LANES), F32)
    for n, part in enumerate(parts):
        aug = jnp.where(lane == base + n, part, aug)
    return jnp.where(own, x.astype(F32), aug).astype(BF16)


def _attn_kernel(q_ref, k_ref, v_ref, cum_ref, o_ref, kaug_ref, acc_ref, *, tq):
    pair = pl.program_id(1)
    i = pl.program_id(2)
    seq = k_ref.shape[1]
    scale = HEAD_DIM ** -0.5

    @pl.when(i == 0)
    def _():
        for hh in range(2):
            for r in range(seq // tq):
                rows = pl.ds(r * tq, tq)
                c = _head_cum(cum_ref[0, rows, :], 2 * pair + hh)
                kaug_ref[hh, rows, :] = _augment(k_ref[0, rows, :], c, hh, False)

    q0 = pl.multiple_of(i * tq, tq)
    cum_q = cum_ref[0, pl.ds(q0, tq), :]
    q_scaled = q_ref[0].astype(F32) * scale
    q_aug = [_augment(q_scaled, _head_cum(cum_q, 2 * pair + hh), hh, True) for hh in range(2)]
    acc_ref[...] = jnp.zeros_like(acc_ref)

    def step(j, carry, masked):
        k0 = pl.multiple_of(j * tq, tq)
        vj = v_ref[0, pl.ds(k0, tq), :]
        out = []
        for hh in range(2):
            m_prev, l_prev = carry[2 * hh], carry[2 * hh + 1]
            kj = kaug_ref[hh, pl.ds(k0, tq), :]
            s = lax.dot_general(q_aug[hh], kj, (((1,), (1,)), ((), ())),
                                preferred_element_type=F32)
            if masked:
                row = lax.broadcasted_iota(jnp.int32, s.shape, 0)
                col = lax.broadcasted_iota(jnp.int32, s.shape, 1)
                s = jnp.where(row >= col, s, NEG_BIG)
            m_new = jnp.maximum(m_prev, jnp.max(s, axis=-1, keepdims=True))
            p = jnp.exp(s - m_new)
            alpha = jnp.exp(m_prev - m_new)
            l_new = alpha * l_prev + jnp.sum(p, axis=-1, keepdims=True)
            acc_ref[hh] = alpha * acc_ref[hh] + jnp.dot(p.astype(BF16), vj,
                                                        preferred_element_type=F32)
            out += [m_new, l_new]
        return tuple(out)

    init = (jnp.full((tq, 1), NEG_BIG, F32), jnp.zeros((tq, 1), F32)) * 2
    carry = lax.fori_loop(0, i, lambda j, c: step(j, c, False), init)
    carry = step(i, carry, True)

    lane = lax.broadcasted_iota(jnp.int32, (tq, LANES), 1)
    o = jnp.where(lane < HEAD_DIM, acc_ref[0] / carry[1], acc_ref[1] / carry[3])
    o_ref[0] = o.astype(o_ref.dtype)


def _attention(q, k, v, cum, tq):
    bsz, seq, _ = q.shape
    grid = (bsz, ATTN_HEADS // 2, seq // tq)
    return pl.pallas_call(
        functools.partial(_attn_kernel, tq=tq),
        grid=grid,
        in_specs=[
            pl.BlockSpec((1, tq, LANES), lambda b, p, i: (b, i, p)),
            pl.BlockSpec((1, seq, LANES), lambda b, p, i: (b, 0, p)),
            pl.BlockSpec((1, seq, LANES), lambda b, p, i: (b, 0, p)),
            pl.BlockSpec((1, seq, LANES), lambda b, p, i: (b, 0, 0)),
        ],
        out_specs=pl.BlockSpec((1, tq, LANES), lambda b, p, i: (b, i, p)),
        out_shape=jax.ShapeDtypeStruct((bsz, seq, ATTN_W), BF16),
        scratch_shapes=[pltpu.VMEM((2, seq, LANES), BF16), pltpu.VMEM((2, tq, LANES), F32)],
        compiler_params=pltpu.CompilerParams(
            dimension_semantics=("arbitrary", "arbitrary", "arbitrary"),
            vmem_limit_bytes=VMEM_LIMIT),
        name="fox_attention",
    )(q, k, v, cum)


def _merge_kernel(a_ref, halo_ref, o_ref, ga_ref, gb_ref, x_ref, mod_ref,
                  cw_ref, cb_ref, lcg_ref, lcb_ref, wc_ref, wa_ref, wm_ref, bm_ref,
                  g1_ref, b1_ref, out_ref, ext_ref, *, d, alpha, halo):
    i = pl.program_id(1)
    tm = a_ref.shape[1]

    @pl.when(i == 0)
    def _():
        ext_ref[0:halo, :] = jnp.zeros((halo, CONV_CH), F32)

    @pl.when(i > 0)
    def _():
        ext_ref[0:halo, :] = halo_ref[0].astype(F32)

    ext_ref[halo:halo + tm, :] = a_ref[0].astype(F32)

    first = halo - (CONV_WIDTH - 1)
    conv = jnp.zeros((tm, CONV_CH), F32) + cb_ref[...]
    for tap in range(CONV_WIDTH):
        conv = conv + cw_ref[tap:tap + 1, :] * ext_ref[first + tap:first + tap + tm, :]

    act = _normalize(conv) * lcg_ref[...] + lcb_ref[...]
    act = (act * _sigmoid(act)).astype(BF16)
    y_a = jnp.dot(act, wc_ref[...], preferred_element_type=F32)
    y_b = jnp.dot(o_ref[0], wa_ref[...], preferred_element_type=F32)
    merged = (ga_ref[0].astype(F32) * y_a + gb_ref[0].astype(F32) * y_b).astype(BF16)
    mix = jnp.dot(merged, wm_ref[...], preferred_element_type=F32) + bm_ref[...]
    gate1 = mod_ref[0, :, 2 * d:3 * d]
    y = alpha * x_ref[0] + (1.0 + gate1) * mix
    out_ref[0] = _normalize(y) * g1_ref[...] + b1_ref[...]


def _merge(a, o, ga, gb, x, mod, cw, cb, lcg, lcb, wc, wa, wm, bm, g1, b1, tm, alpha):
    bsz, seq, d = x.shape
    halo = 32
    assert halo >= CONV_WIDTH - 1 and tm % halo == 0
    grid = (bsz, seq // tm)
    row_spec = lambda w: pl.BlockSpec((1, tm, w), lambda b, i: (b, i, 0))
    halo_spec = pl.BlockSpec(
        (1, halo, CONV_CH), lambda b, i: (b, jnp.maximum(i * (tm // halo) - 1, 0), 0))
    consts = (cw, cb, lcg, lcb, wc, wa, wm, bm, g1, b1)
    return pl.pallas_call(
        functools.partial(_merge_kernel, d=d, alpha=alpha, halo=halo),
        grid=grid,
        in_specs=[row_spec(CONV_CH), halo_spec, row_spec(ATTN_W), row_spec(d), row_spec(d),
                  row_spec(d), pl.BlockSpec((1, 1, mod.shape[-1]), lambda b, i: (b, 0, 0))]
                 + [_const_spec(t.shape) for t in consts],
        out_specs=row_spec(d),
        out_shape=jax.ShapeDtypeStruct((bsz, seq, d), F32),
        scratch_shapes=[pltpu.VMEM((halo + tm, CONV_CH), F32)],
        compiler_params=pltpu.CompilerParams(
            dimension_semantics=("arbitrary", "arbitrary"), vmem_limit_bytes=VMEM_LIMIT),
        name="conv_merge",
    )(a, a, o, ga, gb, x, mod, *consts)


def _ffn_kernel(x_ref, mod_ref, wu_ref, fw_ref, fb_ref, wd_ref, g2_ref, b2_ref,
                out_ref, hbuf_ref, tail_ref, f_ref, *, d, alpha, hc):
    i = pl.program_id(1)
    tm = x_ref.shape[1]
    pad = SUBLANES
    x = x_ref[0]
    shift = mod_ref[0, :, 3 * d:4 * d]
    scale = mod_ref[0, :, 4 * d:5 * d]
    gate2 = mod_ref[0, :, 5 * d:6 * d]
    u = (_normalize(x) * (1.0 + scale) + shift).astype(BF16)

    @pl.when(i == 0)
    def _():
        tail_ref[...] = jnp.zeros_like(tail_ref)

    def conv_chunk(slot, c0):
        h_pre = jnp.dot(u, wu_ref[:, c0:c0 + hc], preferred_element_type=F32)
        hbuf_ref[slot, 0:pad, :] = tail_ref[:, c0:c0 + hc]
        hbuf_ref[slot, pad:pad + tm, :] = h_pre
        tail_ref[:, c0:c0 + hc] = h_pre[tm - pad:tm, :]
        h = jnp.zeros((tm, hc), F32) + fb_ref[:, c0:c0 + hc]
        for tap in range(FFN_CONV_WIDTH):
            off = pad - (FFN_CONV_WIDTH - 1) + tap
            h = h + fw_ref[tap:tap + 1, c0:c0 + hc] * hbuf_ref[slot, off:off + tm, :]
        return h

    for c in range(D_FF // hc):
        h_act = conv_chunk(0, c * hc)
        h_lin = conv_chunk(1, D_FF + c * hc)
        gelu = 0.5 * h_act * (1.0 + lax.erf(h_act * (2.0 ** -0.5)))
        f_ref[:, c * hc:(c + 1) * hc] = (gelu * h_lin).astype(BF16)

    ffn = jnp.dot(f_ref[...], wd_ref[...], preferred_element_type=F32)
    y = alpha * x + (1.0 + gate2) * ffn
    out_ref[0] = _normalize(y) * g2_ref[...] + b2_ref[...]


def _ffn(x, mod, wu, fw, fb, wd, g2, b2, tm, alpha):
    bsz, seq, d = x.shape
    hc = 256
    assert D_FF % hc == 0
    grid = (bsz, seq // tm)
    row_spec = pl.BlockSpec((1, tm, d), lambda b, i: (b, i, 0))
    consts = (wu, fw, fb, wd, g2, b2)
    return pl.pallas_call(
        functools.partial(_ffn_kernel, d=d, alpha=alpha, hc=hc),
        grid=grid,
        in_specs=[row_spec, pl.BlockSpec((1, 1, mod.shape[-1]), lambda b, i: (b, 0, 0))]
                 + [_const_spec(t.shape) for t in consts],
        out_specs=row_spec,
        out_shape=jax.ShapeDtypeStruct((bsz, seq, d), F32),
        scratch_shapes=[pltpu.VMEM((2, SUBLANES + tm, hc), F32),
                        pltpu.VMEM((SUBLANES, 2 * D_FF), F32),
                        pltpu.VMEM((tm, D_FF), BF16)],
        compiler_params=pltpu.CompilerParams(
            dimension_semantics=("arbitrary", "arbitrary"), vmem_limit_bytes=VMEM_LIMIT),
        name="conv_ffn",
    )(x, mod, *consts)


def _pack_inproj(w_in, b_in, d):
    g_off = 2 * CONV_CH + 3 * ATTN_W
    pad = LANES - ATTN_HEADS
    w_main = jnp.concatenate(
        [w_in[:, :g_off + ATTN_HEADS], jnp.zeros((d, pad), w_in.dtype)], axis=1).astype(BF16)
    b_main = jnp.concatenate([b_in[:g_off + ATTN_HEADS], jnp.zeros((pad,), b_in.dtype)])[None, :]
    w_gate = w_in[:, g_off + ATTN_HEADS:].astype(BF16)
    b_gate = b_in[g_off + ATTN_HEADS:][None, :]
    return w_main, b_main, w_gate, b_gate


def kernel(x, c, w_ada, b_ada, w_in, b_in, conv_a_w, conv_a_b, ln_conv_g, ln_conv_b,
           w_conv_proj, w_attn_proj, w_mix_out, b_mix_out, ln1_g, ln1_b,
           w_ffn_up, ffn_conv_w, ffn_conv_b, w_ffn_down, ln2_g, ln2_b):
    depth = w_ada.shape[0]
    bsz, seq, d = x.shape
    alpha = (2.0 * depth) ** 0.25
    tm = min(512, seq)
    tq = min(256, seq)
    mod_all = _modulation(c, w_ada, b_ada)
    row = lambda t: t[None, :]
    for l in range(depth):
        mod = mod_all[l][:, None, :]
        w_main, b_main, w_gate, b_gate = _pack_inproj(w_in[l], b_in[l], d)
        a, q, k, v, ga, gb, cum = _inproj(x, mod, w_main, b_main, w_gate, b_gate, tm)
        o = _attention(q, k, v, cum, tq)
        x = _merge(a, o, ga, gb, x, mod, conv_a_w[l], row(conv_a_b[l]), row(ln_conv_g[l]),
                   row(ln_conv_b[l]), w_conv_proj[l].astype(BF16), w_attn_proj[l].astype(BF16),
                   w_mix_out[l].astype(BF16), row(b_mix_out[l]), row(ln1_g[l]), row(ln1_b[l]),
                   tm, alpha)
        x = _ffn(x, mod, w_ffn_up[l].astype(BF16), ffn_conv_w[l], row(ffn_conv_b[l]),
                 w_ffn_down[l].astype(BF16), row(ln2_g[l]), row(ln2_b[l]), tm, alpha)
    return x
```

```python
import functools
import math

import jax
import jax.numpy as jnp
from jax import lax
from jax.experimental import pallas as pl
from jax.experimental.pallas import tpu as pltpu

CONV_CH = 512
CONV_WIDTH = 31
ATTN_HEADS = 8
HEAD_DIM = 64
ATTN_W = ATTN_HEADS * HEAD_DIM
D_FF = 2816
FFN_CONV_WIDTH = 3
LN_EPS = 1e-5

LANES = 128
SUBLANES = 8
VMEM_LIMIT = 56 * 1024 * 1024

P_GLU = 0
P_Q = P_GLU + 2 * CONV_CH
P_K = P_Q + ATTN_W
P_V = P_K + ATTN_W
P_F = P_V + ATTN_W

F32 = jnp.float32
BF16 = jnp.bfloat16
NEG_BIG = -1e30
LOG2E = math.log2(math.e)
Q_SCALE = HEAD_DIM ** -0.5 * LOG2E


def _const_spec(shape):
    nd = len(shape)
    return pl.BlockSpec(shape, lambda *_: (0,) * nd, pipeline_mode=pl.Buffered(1))


def _normalize(x):
    mu = jnp.mean(x, axis=-1, keepdims=True)
    xc = x - mu
    var = jnp.mean(xc * xc, axis=-1, keepdims=True)
    return xc * lax.rsqrt(var + LN_EPS)


def _sigmoid(x):
    return jax.nn.sigmoid(x)


def _split3(x):
    hi = x.astype(BF16)
    r1 = x - hi.astype(F32)
    mid = r1.astype(BF16)
    lo = (r1 - mid.astype(F32)).astype(BF16)
    return hi, mid, lo


def _mod_kernel(c_ref, w_ref, b_ref, o_ref):
    c = c_ref[...]
    c_act = c * _sigmoid(c)
    o_ref[0] = jnp.dot(c_act, w_ref[0], preferred_element_type=F32,
                       precision=lax.Precision.HIGHEST) + b_ref[0]


def _modulation(c, w_ada, b_ada):
    depth, d, n = w_ada.shape
    bsz = c.shape[0]
    tn = n // 6
    return pl.pallas_call(
        _mod_kernel,
        grid=(depth, n // tn),
        in_specs=[
            pl.BlockSpec((bsz, d), lambda l, j: (0, 0)),
            pl.BlockSpec((1, d, tn), lambda l, j: (l, 0, j)),
            pl.BlockSpec((1, 1, tn), lambda l, j: (l, 0, j)),
        ],
        out_specs=pl.BlockSpec((1, bsz, tn), lambda l, j: (l, 0, j)),
        out_shape=jax.ShapeDtypeStruct((depth, bsz, n), F32),
        compiler_params=pltpu.CompilerParams(
            dimension_semantics=("arbitrary", "arbitrary"), vmem_limit_bytes=VMEM_LIMIT),
        name="adaln_modulation",
    )(c, w_ada, b_ada.reshape(depth, 1, n))


def _inproj_kernel(x_ref, mod_ref, w_ref, b_ref, wg_ref, bg_ref,
                   a_ref, q_ref, k_ref, v_ref, ga_ref, gb_ref, aq_ref, ak_ref, carry_ref, *, d):
    i = pl.program_id(1)
    tm = x_ref.shape[1]
    shift = mod_ref[0, :, 0:d]
    scale = mod_ref[0, :, d:2 * d]
    u = (_normalize(x_ref[0]) * (1.0 + scale) + shift).astype(BF16)

    def proj(lo, hi):
        return jnp.dot(u, w_ref[:, lo:hi], preferred_element_type=F32) + b_ref[:, lo:hi]

    val = proj(P_GLU, P_GLU + CONV_CH)
    gate = proj(P_GLU + CONV_CH, P_Q)
    a_ref[0] = (val * _sigmoid(gate)).astype(BF16)
    q_ref[0] = (proj(P_Q, P_K) * Q_SCALE).astype(BF16)
    k_ref[0] = proj(P_K, P_V).astype(BF16)
    v_ref[0] = proj(P_V, P_F).astype(BF16)
    ga_ref[0] = _sigmoid(jnp.dot(u, wg_ref[:, 0:d], preferred_element_type=F32)
                         + bg_ref[:, 0:d]).astype(BF16)
    gb_ref[0] = _sigmoid(jnp.dot(u, wg_ref[:, d:2 * d], preferred_element_type=F32)
                         + bg_ref[:, d:2 * d]).astype(BF16)

    zf = proj(P_F, P_F + LANES)
    log_f = jnp.minimum(zf, 0.0) - jnp.log1p(jnp.exp(-jnp.abs(zf)))

    @pl.when(i == 0)
    def _():
        carry_ref[...] = jnp.zeros_like(carry_ref)

    row = lax.broadcasted_iota(jnp.int32, (LANES, LANES), 0)
    col = lax.broadcasted_iota(jnp.int32, (LANES, LANES), 1)
    tri = jnp.where(row >= col, 1.0, 0.0).astype(BF16)
    group = col // ATTN_HEADS
    carry = carry_ref[0:1, :]
    for r in range(tm // LANES):
        rows = slice(r * LANES, (r + 1) * LANES)
        hi, mid, lo = _split3(log_f[rows, :])
        cs = (jnp.dot(tri, hi, preferred_element_type=F32)
              + jnp.dot(tri, mid, preferred_element_type=F32)
              + jnp.dot(tri, lo, preferred_element_type=F32)) + carry
        carry = cs[LANES - 1:LANES, :]
        parts = [p.astype(F32) for p in _split3(cs * LOG2E)]
        ak = jnp.where(group < 6, 1.0, 0.0)
        aq = jnp.where(group < 3, 1.0, 0.0)
        for g, part in enumerate(parts):
            placed_k = part if g == 0 else pltpu.roll(part, g * ATTN_HEADS, 1)
            ak = jnp.where(group == g, -placed_k, ak)
            aq = jnp.where(group == 3 + g, pltpu.roll(part, (3 + g) * ATTN_HEADS, 1), aq)
        ak_ref[0, rows, :] = ak.astype(BF16)
        aq_ref[0, rows, :] = aq.astype(BF16)
    carry_ref[0:1, :] = carry


def _inproj(x, mod, w_main, b_main, w_gate, b_gate, tm):
    bsz, seq, d = x.shape
    n_main = w_main.shape[1]
    grid = (bsz, seq // tm)
    row_spec = lambda w: pl.BlockSpec((1, tm, w), lambda b, i: (b, i, 0))
    out_shape = (
        jax.ShapeDtypeStruct((bsz, seq, CONV_CH), BF16),
        jax.ShapeDtypeStruct((bsz, seq, ATTN_W), BF16),
        jax.ShapeDtypeStruct((bsz, seq, ATTN_W), BF16),
        jax.ShapeDtypeStruct((bsz, seq, ATTN_W), BF16),
        jax.ShapeDtypeStruct((bsz, seq, d), BF16),
        jax.ShapeDtypeStruct((bsz, seq, d), BF16),
        jax.ShapeDtypeStruct((bsz, seq, LANES), BF16),
        jax.ShapeDtypeStruct((bsz, seq, LANES), BF16),
    )
    return pl.pallas_call(
        functools.partial(_inproj_kernel, d=d),
        grid=grid,
        in_specs=[
            row_spec(d),
            pl.BlockSpec((1, 1, mod.shape[-1]), lambda b, i: (b, 0, 0)),
            _const_spec((d, n_main)),
            _const_spec((1, n_main)),
            _const_spec((d, 2 * d)),
            _const_spec((1, 2 * d)),
        ],
        out_specs=(row_spec(CONV_CH), row_spec(ATTN_W), row_spec(ATTN_W), row_spec(ATTN_W),
                   row_spec(d), row_spec(d), row_spec(LANES), row_spec(LANES)),
        out_shape=out_shape,
        scratch_shapes=[pltpu.VMEM((SUBLANES, LANES), F32)],
        compiler_params=pltpu.CompilerParams(
            dimension_semantics=("arbitrary", "arbitrary"), vmem_limit_bytes=VMEM_LIMIT),
        name="inproj",
    )(x, mod, w_main, b_main, w_gate, b_gate)


def _attn_kernel(q_ref, aq_ref, k_ref, v_ref, ak_ref, o_ref, qa_ref, m_ref, acc_ref, *, tq):
    i = pl.program_id(1)
    lane = lax.broadcasted_iota(jnp.int32, (tq, LANES), 1)
    low_half = lane < HEAD_DIM
    own = (low_half, jnp.logical_not(low_half))
    zero = jnp.zeros((tq, LANES), BF16)
    one = jnp.ones((tq, LANES), BF16)

    aq = aq_ref[0]
    for h in range(ATTN_HEADS):
        pair, rows = h // 2, slice((h % 2) * tq, (h % 2 + 1) * tq)
        qa_ref[pair, rows, 0:LANES] = jnp.where(
            own[h % 2], q_ref[0, :, pair * LANES:(pair + 1) * LANES], zero)
        qa_ref[pair, rows, LANES:2 * LANES] = jnp.where(lane % ATTN_HEADS == h, aq, zero)
    m_ref[...] = jnp.full_like(m_ref, NEG_BIG)
    acc_ref[...] = jnp.zeros_like(acc_ref)

    def attend(start, diagonal):
        rows = pl.ds(start, tq)
        ak = ak_ref[0, rows, :]
        if diagonal:
            row = lax.broadcasted_iota(jnp.int32, (tq, tq), 0)
            col = lax.broadcasted_iota(jnp.int32, (tq, tq), 1)
            causal = row >= col
        for pair in range(ATTN_HEADS // 2):
            blk = slice(pair * LANES, (pair + 1) * LANES)
            kk = jnp.concatenate([k_ref[0, rows, blk], ak], axis=1)
            vv = v_ref[0, rows, blk]
            s_pair = lax.dot_general(qa_ref[pair], kk, (((1,), (1,)), ((), ())),
                                     preferred_element_type=F32)
            for e in range(2):
                h = 2 * pair + e
                s = s_pair[e * tq:(e + 1) * tq, :]
                if diagonal:
                    s = jnp.where(causal, s, NEG_BIG)
                m_prev = m_ref[h]
                m_new = jnp.maximum(m_prev, jnp.max(s, axis=-1, keepdims=True))
                p = jnp.concatenate(
                    [jnp.exp2(s[:, c * LANES:(c + 1) * LANES] - m_new)
                     for c in range(tq // LANES)], axis=1).astype(BF16)
                pv = jnp.dot(p, jnp.where(own[e], vv, one), preferred_element_type=F32)
                acc_ref[h] = jnp.exp2(m_prev - m_new) * acc_ref[h] + pv
                m_ref[h] = m_new

    def full_block(j, carry):
        attend(pl.multiple_of(j * tq, tq), False)
        return carry

    lax.fori_loop(0, i, full_block, 0)
    attend(pl.multiple_of(i * tq, tq), True)

    for pair in range(ATTN_HEADS // 2):
        acc0, acc1 = acc_ref[2 * pair], acc_ref[2 * pair + 1]
        o = jnp.where(low_half, acc0 / pltpu.roll(acc0, HEAD_DIM, 1),
                      acc1 / pltpu.roll(acc1, HEAD_DIM, 1))
        o_ref[0, :, pair * LANES:(pair + 1) * LANES] = o.astype(o_ref.dtype)


def _attention(q, aq, k, v, ak, tq):
    bsz, seq, _ = q.shape
    grid = (bsz, seq // tq)
    q_spec = lambda w: pl.BlockSpec((1, tq, w), lambda b, i: (b, i, 0))
    kv_spec = lambda w: pl.BlockSpec((1, seq, w), lambda b, i: (b, 0, 0))
    return pl.pallas_call(
        functools.partial(_attn_kernel, tq=tq),
        grid=grid,
        in_specs=[q_spec(ATTN_W), q_spec(LANES), kv_spec(ATTN_W), kv_spec(ATTN_W), kv_spec(LANES)],
        out_specs=q_spec(ATTN_W),
        out_shape=jax.ShapeDtypeStruct((bsz, seq, ATTN_W), BF16),
        scratch_shapes=[pltpu.VMEM((ATTN_HEADS // 2, 2 * tq, 2 * LANES), BF16),
                        pltpu.VMEM((ATTN_HEADS, tq, LANES), F32),
                        pltpu.VMEM((ATTN_HEADS, tq, LANES), F32)],
        compiler_params=pltpu.CompilerParams(
            dimension_semantics=("arbitrary", "arbitrary"), vmem_limit_bytes=VMEM_LIMIT),
        name="fox_attention",
    )(q, aq, k, v, ak)


def _merge_kernel(a_ref, halo_ref, o_ref, ga_ref, gb_ref, x_ref, mod_ref,
                  cw_ref, cb_ref, lcg_ref, lcb_ref, wc_ref, wa_ref, wm_ref, bm_ref,
                  g1_ref, b1_ref, out_ref, shift_ref, conv_ref, *, d, alpha, halo):
    i = pl.program_id(1)
    tm = a_ref.shape[1]
    ext_rows = halo + tm
    first = halo - (CONV_WIDTH - 1)
    row_block = 128

    for lc in range(CONV_CH // LANES):
        lanes = slice(lc * LANES, (lc + 1) * LANES)
        left = jnp.where(i > 0, halo_ref[0, :, lanes].astype(F32), 0.0)
        ext = jnp.concatenate([left, a_ref[0, :, lanes].astype(F32)], axis=0)
        for r in range(SUBLANES):
            shift_ref[r] = ext if r == 0 else pltpu.roll(ext, ext_rows - r, 0)
        for rb in range(tm // row_block):
            acc = jnp.zeros((row_block, LANES), F32) + cb_ref[:, lanes]
            for tap in range(CONV_WIDTH):
                off = first + tap
                base = (off // SUBLANES) * SUBLANES + rb * row_block
                acc = acc + cw_ref[tap:tap + 1, lanes] * shift_ref[off % SUBLANES,
                                                                   base:base + row_block, :]
            conv_ref[rb * row_block:(rb + 1) * row_block, lanes] = acc
    conv = conv_ref[...]

    act = _normalize(conv) * lcg_ref[...] + lcb_ref[...]
    act = (act * _sigmoid(act)).astype(BF16)
    y_a = jnp.dot(act, wc_ref[...], preferred_element_type=F32)
    y_b = jnp.dot(o_ref[0], wa_ref[...], preferred_element_type=F32)
    merged = (ga_ref[0].astype(F32) * y_a + gb_ref[0].astype(F32) * y_b).astype(BF16)
    mix = jnp.dot(merged, wm_ref[...], preferred_element_type=F32) + bm_ref[...]
    gate1 = mod_ref[0, :, 2 * d:3 * d]
    y = alpha * x_ref[0] + (1.0 + gate1) * mix
    out_ref[0] = _normalize(y) * g1_ref[...] + b1_ref[...]


def _merge(a, o, ga, gb, x, mod, cw, cb, lcg, lcb, wc, wa, wm, bm, g1, b1, tm, alpha):
    bsz, seq, d = x.shape
    halo = 32
    assert halo >= CONV_WIDTH - 1 and tm % halo == 0
    grid = (bsz, seq // tm)
    row_spec = lambda w: pl.BlockSpec((1, tm, w), lambda b, i: (b, i, 0))
    halo_spec = pl.BlockSpec(
        (1, halo, CONV_CH), lambda b, i: (b, jnp.maximum(i * (tm // halo) - 1, 0), 0))
    consts = (cw, cb, lcg, lcb, wc, wa, wm, bm, g1, b1)
    return pl.pallas_call(
        functools.partial(_merge_kernel, d=d, alpha=alpha, halo=halo),
        grid=grid,
        in_specs=[row_spec(CONV_CH), halo_spec, row_spec(ATTN_W), row_spec(d), row_spec(d),
                  row_spec(d), pl.BlockSpec((1, 1, mod.shape[-1]), lambda b, i: (b, 0, 0))]
                 + [_const_spec(t.shape) for t in consts],
        out_specs=row_spec(d),
        out_shape=jax.ShapeDtypeStruct((bsz, seq, d), F32),
        scratch_shapes=[pltpu.VMEM((SUBLANES, halo + tm, LANES), F32),
                        pltpu.VMEM((tm, CONV_CH), F32)],
        compiler_params=pltpu.CompilerParams(
            dimension_semantics=("arbitrary", "arbitrary"), vmem_limit_bytes=VMEM_LIMIT),
        name="conv_merge",
    )(a, a, o, ga, gb, x, mod, *consts)


def _ffn_kernel(x_ref, mod_ref, wu_ref, fw_ref, fb_ref, wd_ref, g2_ref, b2_ref,
                out_ref, tail_ref, f_ref, *, d, alpha, hc):
    i = pl.program_id(1)
    tm = x_ref.shape[1]
    pad = SUBLANES
    x = x_ref[0]
    shift = mod_ref[0, :, 3 * d:4 * d]
    scale = mod_ref[0, :, 4 * d:5 * d]
    gate2 = mod_ref[0, :, 5 * d:6 * d]
    u = (_normalize(x) * (1.0 + scale) + shift).astype(BF16)

    @pl.when(i == 0)
    def _():
        tail_ref[...] = jnp.zeros_like(tail_ref)

    def conv_chunk(c0):
        cols = slice(c0, c0 + hc)
        h_pre = jnp.dot(u, wu_ref[:, cols], preferred_element_type=F32)
        prev = tail_ref[:, cols]
        tail_ref[:, cols] = h_pre[tm - pad:tm, :]
        w0, w1, w2 = (fw_ref[tap:tap + 1, cols] for tap in range(FFN_CONV_WIDTH))
        bias = fb_ref[:, cols]
        body = bias + w2 * h_pre + w1 * pltpu.roll(h_pre, 1, 0) + w0 * pltpu.roll(h_pre, 2, 0)
        top = jnp.concatenate([prev, h_pre[0:pad, :]], axis=0)
        head = (bias + w2 * top[pad:2 * pad, :] + w1 * top[pad - 1:2 * pad - 1, :]
                + w0 * top[pad - 2:2 * pad - 2, :])
        return jnp.concatenate([head, body[pad:, :]], axis=0)

    for c in range(D_FF // hc):
        h_act = conv_chunk(c * hc)
        h_lin = conv_chunk(D_FF + c * hc)
        gelu = 0.5 * h_act * (1.0 + lax.erf(h_act * (2.0 ** -0.5)))
        f_ref[:, c * hc:(c + 1) * hc] = (gelu * h_lin).astype(BF16)

    ffn = jnp.dot(f_ref[...], wd_ref[...], preferred_element_type=F32)
    y = alpha * x + (1.0 + gate2) * ffn
    out_ref[0] = _normalize(y) * g2_ref[...] + b2_ref[...]


def _ffn(x, mod, wu, fw, fb, wd, g2, b2, tm, alpha):
    bsz, seq, d = x.shape
    hc = 256
    assert D_FF % hc == 0 and FFN_CONV_WIDTH == 3
    grid = (bsz, seq // tm)
    row_spec = pl.BlockSpec((1, tm, d), lambda b, i: (b, i, 0))
    consts = (wu, fw, fb, wd, g2, b2)
    return pl.pallas_call(
        functools.partial(_ffn_kernel, d=d, alpha=alpha, hc=hc),
        grid=grid,
        in_specs=[row_spec, pl.BlockSpec((1, 1, mod.shape[-1]), lambda b, i: (b, 0, 0))]
                 + [_const_spec(t.shape) for t in consts],
        out_specs=row_spec,
        out_shape=jax.ShapeDtypeStruct((bsz, seq, d), F32),
        scratch_shapes=[pltpu.VMEM((SUBLANES, 2 * D_FF), F32),
                        pltpu.VMEM((tm, D_FF), BF16)],
        compiler_params=pltpu.CompilerParams(
            dimension_semantics=("arbitrary", "arbitrary"), vmem_limit_bytes=VMEM_LIMIT),
        name="conv_ffn",
    )(x, mod, *consts)


def _pack_inproj(w_in, b_in, d):
    g_off = 2 * CONV_CH + 3 * ATTN_W
    pad = LANES - ATTN_HEADS
    w_main = jnp.concatenate(
        [w_in[:, :g_off + ATTN_HEADS], jnp.zeros((d, pad), w_in.dtype)], axis=1).astype(BF16)
    b_main = jnp.concatenate([b_in[:g_off + ATTN_HEADS], jnp.zeros((pad,), b_in.dtype)])[None, :]
    w_gate = w_in[:, g_off + ATTN_HEADS:].astype(BF16)
    b_gate = b_in[g_off + ATTN_HEADS:][None, :]
    return w_main, b_main, w_gate, b_gate


def kernel(x, c, w_ada, b_ada, w_in, b_in, conv_a_w, conv_a_b, ln_conv_g, ln_conv_b,
           w_conv_proj, w_attn_proj, w_mix_out, b_mix_out, ln1_g, ln1_b,
           w_ffn_up, ffn_conv_w, ffn_conv_b, w_ffn_down, ln2_g, ln2_b):
    depth = w_ada.shape[0]
    bsz, seq, d = x.shape
    alpha = (2.0 * depth) ** 0.25
    tm = min(512, seq)
    tq = min(512, seq)
    mod_all = _modulation(c, w_ada, b_ada)
    row = lambda t: t[None, :]
    for l in range(depth):
        mod = mod_all[l][:, None, :]
        w_main, b_main, w_gate, b_gate = _pack_inproj(w_in[l], b_in[l], d)
        a, q, k, v, ga, gb, aq, ak = _inproj(x, mod, w_main, b_main, w_gate, b_gate, tm)
        o = _attention(q, aq, k, v, ak, tq)
        x = _merge(a, o, ga, gb, x, mod, conv_a_w[l], row(conv_a_b[l]), row(ln_conv_g[l]),
                   row(ln_conv_b[l]), w_conv_proj[l].astype(BF16), w_attn_proj[l].astype(BF16),
                   w_mix_out[l].astype(BF16), row(b_mix_out[l]), row(ln1_g[l]), row(ln1_b[l]),
                   tm, alpha)
        x = _ffn(x, mod, w_ffn_up[l].astype(BF16), ffn_conv_w[l], row(ffn_conv_b[l]),
                 w_ffn_down[l].astype(BF16), row(ln2_g[l]), row(ln2_b[l]), tm, alpha)
    return x
```

```python
import functools
import math

import jax
import jax.numpy as jnp
from jax import lax
from jax.experimental import pallas as pl
from jax.experimental.pallas import tpu as pltpu

CONV_CH = 512
CONV_WIDTH = 31
ATTN_HEADS = 8
HEAD_DIM = 64
ATTN_W = ATTN_HEADS * HEAD_DIM
D_FF = 2816
FFN_CONV_WIDTH = 3
LN_EPS = 1e-5

LANES = 128
SUBLANES = 8
VMEM_LIMIT = 56 * 1024 * 1024

P_GLU = 0
P_Q = P_GLU + 2 * CONV_CH
P_K = P_Q + ATTN_W
P_V = P_K + ATTN_W
P_F = P_V + ATTN_W

F32 = jnp.float32
BF16 = jnp.bfloat16
NEG_BIG = -1e30
FFN_SUB_ROWS = 512
LOG2E = math.log2(math.e)
Q_SCALE = HEAD_DIM ** -0.5 * LOG2E


def _const_spec(shape):
    nd = len(shape)
    return pl.BlockSpec(shape, lambda *_: (0,) * nd, pipeline_mode=pl.Buffered(1))


def _normalize(x):
    mu = jnp.mean(x, axis=-1, keepdims=True)
    xc = x - mu
    var = jnp.mean(xc * xc, axis=-1, keepdims=True)
    return xc * lax.rsqrt(var + LN_EPS)


def _sigmoid(x):
    return jax.nn.sigmoid(x)


def _split3(x):
    hi = x.astype(BF16)
    r1 = x - hi.astype(F32)
    mid = r1.astype(BF16)
    lo = (r1 - mid.astype(F32)).astype(BF16)
    return hi, mid, lo


def _mod_kernel(c_ref, w_ref, b_ref, o_ref):
    c = c_ref[...]
    c_act = c * _sigmoid(c)
    o_ref[0] = jnp.dot(c_act, w_ref[0], preferred_element_type=F32,
                       precision=lax.Precision.HIGHEST) + b_ref[0]


def _modulation(c, w_ada, b_ada):
    depth, d, n = w_ada.shape
    bsz = c.shape[0]
    tn = n // 6
    return pl.pallas_call(
        _mod_kernel,
        grid=(depth, n // tn),
        in_specs=[
            pl.BlockSpec((bsz, d), lambda l, j: (0, 0)),
            pl.BlockSpec((1, d, tn), lambda l, j: (l, 0, j)),
            pl.BlockSpec((1, 1, tn), lambda l, j: (l, 0, j)),
        ],
        out_specs=pl.BlockSpec((1, bsz, tn), lambda l, j: (l, 0, j)),
        out_shape=jax.ShapeDtypeStruct((depth, bsz, n), F32),
        compiler_params=pltpu.CompilerParams(
            dimension_semantics=("arbitrary", "arbitrary"), vmem_limit_bytes=VMEM_LIMIT),
        name="adaln_modulation",
    )(c, w_ada, b_ada.reshape(depth, 1, n))


def _inproj_kernel(x_ref, mod_ref, w_ref, b_ref, wg_ref, bg_ref,
                   a_ref, q_ref, k_ref, v_ref, ga_ref, gb_ref, aq_ref, ak_ref, carry_ref, *, d):
    i = pl.program_id(1)
    tm = x_ref.shape[1]
    shift = mod_ref[0, :, 0:d]
    scale = mod_ref[0, :, d:2 * d]
    u = (_normalize(x_ref[0]) * (1.0 + scale) + shift).astype(BF16)

    def proj(lo, hi):
        return jnp.dot(u, w_ref[:, lo:hi], preferred_element_type=F32) + b_ref[:, lo:hi]

    val = proj(P_GLU, P_GLU + CONV_CH)
    gate = proj(P_GLU + CONV_CH, P_Q)
    a_ref[0] = (val * _sigmoid(gate)).astype(BF16)
    q_ref[0] = (proj(P_Q, P_K) * Q_SCALE).astype(BF16)
    k_ref[0] = proj(P_K, P_V).astype(BF16)
    v_ref[0] = proj(P_V, P_F).astype(BF16)
    ga_ref[0] = _sigmoid(jnp.dot(u, wg_ref[:, 0:d], preferred_element_type=F32)
                         + bg_ref[:, 0:d]).astype(BF16)
    gb_ref[0] = _sigmoid(jnp.dot(u, wg_ref[:, d:2 * d], preferred_element_type=F32)
                         + bg_ref[:, d:2 * d]).astype(BF16)

    zf = proj(P_F, P_F + LANES)
    log_f = jnp.minimum(zf, 0.0) - jnp.log1p(jnp.exp(-jnp.abs(zf)))

    @pl.when(i == 0)
    def _():
        carry_ref[...] = jnp.zeros_like(carry_ref)

    row = lax.broadcasted_iota(jnp.int32, (LANES, LANES), 0)
    col = lax.broadcasted_iota(jnp.int32, (LANES, LANES), 1)
    tri = jnp.where(row >= col, 1.0, 0.0).astype(BF16)
    group = col // ATTN_HEADS
    carry = carry_ref[0:1, :]
    for r in range(tm // LANES):
        rows = slice(r * LANES, (r + 1) * LANES)
        hi, mid, lo = _split3(log_f[rows, :])
        cs = (jnp.dot(tri, hi, preferred_element_type=F32)
              + jnp.dot(tri, mid, preferred_element_type=F32)
              + jnp.dot(tri, lo, preferred_element_type=F32)) + carry
        carry = cs[LANES - 1:LANES, :]
        parts = [p.astype(F32) for p in _split3(cs * LOG2E)]
        ak = jnp.where(group < 6, 1.0, 0.0)
        aq = jnp.where(group < 3, 1.0, 0.0)
        for g, part in enumerate(parts):
            placed_k = part if g == 0 else pltpu.roll(part, g * ATTN_HEADS, 1)
            ak = jnp.where(group == g, -placed_k, ak)
            aq = jnp.where(group == 3 + g, pltpu.roll(part, (3 + g) * ATTN_HEADS, 1), aq)
        ak_ref[0, rows, :] = ak.astype(BF16)
        aq_ref[0, rows, :] = aq.astype(BF16)
    carry_ref[0:1, :] = carry


def _inproj(x, mod, w_main, b_main, w_gate, b_gate, tm):
    bsz, seq, d = x.shape
    n_main = w_main.shape[1]
    grid = (bsz, seq // tm)
    row_spec = lambda w: pl.BlockSpec((1, tm, w), lambda b, i: (b, i, 0))
    out_shape = (
        jax.ShapeDtypeStruct((bsz, seq, CONV_CH), BF16),
        jax.ShapeDtypeStruct((bsz, seq, ATTN_W), BF16),
        jax.ShapeDtypeStruct((bsz, seq, ATTN_W), BF16),
        jax.ShapeDtypeStruct((bsz, seq, ATTN_W), BF16),
        jax.ShapeDtypeStruct((bsz, seq, d), BF16),
        jax.ShapeDtypeStruct((bsz, seq, d), BF16),
        jax.ShapeDtypeStruct((bsz, seq, LANES), BF16),
        jax.ShapeDtypeStruct((bsz, seq, LANES), BF16),
    )
    return pl.pallas_call(
        functools.partial(_inproj_kernel, d=d),
        grid=grid,
        in_specs=[
            row_spec(d),
            pl.BlockSpec((1, 1, mod.shape[-1]), lambda b, i: (b, 0, 0)),
            _const_spec((d, n_main)),
            _const_spec((1, n_main)),
            _const_spec((d, 2 * d)),
            _const_spec((1, 2 * d)),
        ],
        out_specs=(row_spec(CONV_CH), row_spec(ATTN_W), row_spec(ATTN_W), row_spec(ATTN_W),
                   row_spec(d), row_spec(d), row_spec(LANES), row_spec(LANES)),
        out_shape=out_shape,
        scratch_shapes=[pltpu.VMEM((SUBLANES, LANES), F32)],
        compiler_params=pltpu.CompilerParams(
            dimension_semantics=("arbitrary", "arbitrary"), vmem_limit_bytes=VMEM_LIMIT),
        name="inproj",
    )(x, mod, w_main, b_main, w_gate, b_gate)


def _attn_kernel(q_ref, aq_ref, k_ref, v_ref, ak_ref, o_ref, qa_ref, m_ref, acc_ref, *, tq):
    i = pl.program_id(1)
    lane = lax.broadcasted_iota(jnp.int32, (tq, LANES), 1)
    low_half = lane < HEAD_DIM
    own = (low_half, jnp.logical_not(low_half))
    zero = jnp.zeros((tq, LANES), BF16)
    one = jnp.ones((tq, LANES), BF16)

    aq = aq_ref[0]
    for h in range(ATTN_HEADS):
        pair, rows = h // 2, slice((h % 2) * tq, (h % 2 + 1) * tq)
        qa_ref[pair, rows, 0:LANES] = jnp.where(
            own[h % 2], q_ref[0, :, pair * LANES:(pair + 1) * LANES], zero)
        qa_ref[pair, rows, LANES:2 * LANES] = jnp.where(lane % ATTN_HEADS == h, aq, zero)

    def attend(start, diagonal):
        rows = pl.ds(start, tq)
        ak = ak_ref[0, rows, :]
        if diagonal:
            row = lax.broadcasted_iota(jnp.int32, (tq, tq), 0)
            col = lax.broadcasted_iota(jnp.int32, (tq, tq), 1)
            causal = row >= col
        for pair in range(ATTN_HEADS // 2):
            blk = slice(pair * LANES, (pair + 1) * LANES)
            kk = jnp.concatenate([k_ref[0, rows, blk], ak], axis=1)
            vv = v_ref[0, rows, blk]
            s_pair = lax.dot_general(qa_ref[pair], kk, (((1,), (1,)), ((), ())),
                                     preferred_element_type=F32)
            for e in range(2):
                h = 2 * pair + e
                s = s_pair[e * tq:(e + 1) * tq, :]
                if diagonal:
                    s = jnp.where(causal, s, NEG_BIG)
                m_new = jnp.max(s, axis=-1, keepdims=True)
                if diagonal:
                    m_new = jnp.broadcast_to(m_new, (tq, LANES))
                else:
                    m_prev = m_ref[h]
                    m_new = jnp.maximum(m_prev, m_new)
                p = jnp.concatenate(
                    [jnp.exp2(s[:, c * LANES:(c + 1) * LANES] - m_new)
                     for c in range(tq // LANES)], axis=1).astype(BF16)
                pv = jnp.dot(p, jnp.where(own[e], vv, one), preferred_element_type=F32)
                if diagonal:
                    acc_ref[h] = pv
                else:
                    acc_ref[h] = jnp.exp2(m_prev - m_new) * acc_ref[h] + pv
                m_ref[h] = m_new

    def full_block(j, carry):
        attend(pl.multiple_of(j * tq, tq), False)
        return carry

    attend(pl.multiple_of(i * tq, tq), True)
    lax.fori_loop(0, i, full_block, 0)

    for pair in range(ATTN_HEADS // 2):
        acc0, acc1 = acc_ref[2 * pair], acc_ref[2 * pair + 1]
        numer = jnp.where(low_half, acc0, acc1)
        denom = pltpu.roll(jnp.where(low_half, acc1, acc0), HEAD_DIM, 1)
        o_ref[0, :, pair * LANES:(pair + 1) * LANES] = (numer / denom).astype(o_ref.dtype)


def _attention(q, aq, k, v, ak, tq):
    bsz, seq, _ = q.shape
    grid = (bsz, seq // tq)
    q_spec = lambda w: pl.BlockSpec((1, tq, w), lambda b, i: (b, i, 0))
    kv_spec = lambda w: pl.BlockSpec((1, seq, w), lambda b, i: (b, 0, 0))
    return pl.pallas_call(
        functools.partial(_attn_kernel, tq=tq),
        grid=grid,
        in_specs=[q_spec(ATTN_W), q_spec(LANES), kv_spec(ATTN_W), kv_spec(ATTN_W), kv_spec(LANES)],
        out_specs=q_spec(ATTN_W),
        out_shape=jax.ShapeDtypeStruct((bsz, seq, ATTN_W), BF16),
        scratch_shapes=[pltpu.VMEM((ATTN_HEADS // 2, 2 * tq, 2 * LANES), BF16),
                        pltpu.VMEM((ATTN_HEADS, tq, LANES), F32),
                        pltpu.VMEM((ATTN_HEADS, tq, LANES), F32)],
        compiler_params=pltpu.CompilerParams(
            dimension_semantics=("arbitrary", "arbitrary"), vmem_limit_bytes=VMEM_LIMIT),
        name="fox_attention",
    )(q, aq, k, v, ak)


def _merge_kernel(a_ref, halo_ref, o_ref, ga_ref, gb_ref, x_ref, mod_ref,
                  cw_ref, cb_ref, lcg_ref, lcb_ref, wc_ref, wa_ref, wm_ref, bm_ref,
                  g1_ref, b1_ref, out_ref, shift_ref, conv_ref, *, d, alpha, halo):
    i = pl.program_id(1)
    tm = a_ref.shape[1]
    ext_rows = halo + tm
    first = halo - (CONV_WIDTH - 1)
    row_block = 128

    for lc in range(CONV_CH // LANES):
        lanes = slice(lc * LANES, (lc + 1) * LANES)
        left = jnp.where(i > 0, halo_ref[0, :, lanes].astype(F32), 0.0)
        ext = jnp.concatenate([left, a_ref[0, :, lanes].astype(F32)], axis=0)
        for r in range(SUBLANES):
            shift_ref[r] = ext if r == 0 else pltpu.roll(ext, ext_rows - r, 0)
        for rb in range(tm // row_block):
            acc = jnp.zeros((row_block, LANES), F32) + cb_ref[:, lanes]
            for tap in range(CONV_WIDTH):
                off = first + tap
                base = (off // SUBLANES) * SUBLANES + rb * row_block
                acc = acc + cw_ref[tap:tap + 1, lanes] * shift_ref[off % SUBLANES,
                                                                   base:base + row_block, :]
            conv_ref[rb * row_block:(rb + 1) * row_block, lanes] = acc
    conv = conv_ref[...]

    act = _normalize(conv) * lcg_ref[...] + lcb_ref[...]
    act = (act * _sigmoid(act)).astype(BF16)
    y_a = jnp.dot(act, wc_ref[...], preferred_element_type=F32)
    y_b = jnp.dot(o_ref[0], wa_ref[...], preferred_element_type=F32)
    merged = (ga_ref[0].astype(F32) * y_a + gb_ref[0].astype(F32) * y_b).astype(BF16)
    mix = jnp.dot(merged, wm_ref[...], preferred_element_type=F32) + bm_ref[...]
    gate1 = mod_ref[0, :, 2 * d:3 * d]
    y = alpha * x_ref[0] + (1.0 + gate1) * mix
    out_ref[0] = _normalize(y) * g1_ref[...] + b1_ref[...]


def _merge(a, o, ga, gb, x, mod, cw, cb, lcg, lcb, wc, wa, wm, bm, g1, b1, tm, alpha):
    bsz, seq, d = x.shape
    halo = 32
    assert halo >= CONV_WIDTH - 1 and tm % halo == 0
    grid = (bsz, seq // tm)
    row_spec = lambda w: pl.BlockSpec((1, tm, w), lambda b, i: (b, i, 0))
    halo_spec = pl.BlockSpec(
        (1, halo, CONV_CH), lambda b, i: (b, jnp.maximum(i * (tm // halo) - 1, 0), 0))
    consts = (cw, cb, lcg, lcb, wc, wa, wm, bm, g1, b1)
    return pl.pallas_call(
        functools.partial(_merge_kernel, d=d, alpha=alpha, halo=halo),
        grid=grid,
        in_specs=[row_spec(CONV_CH), halo_spec, row_spec(ATTN_W), row_spec(d), row_spec(d),
                  row_spec(d), pl.BlockSpec((1, 1, mod.shape[-1]), lambda b, i: (b, 0, 0))]
                 + [_const_spec(t.shape) for t in consts],
        out_specs=row_spec(d),
        out_shape=jax.ShapeDtypeStruct((bsz, seq, d), F32),
        scratch_shapes=[pltpu.VMEM((SUBLANES, halo + tm, LANES), F32),
                        pltpu.VMEM((tm, CONV_CH), F32)],
        compiler_params=pltpu.CompilerParams(
            dimension_semantics=("arbitrary", "arbitrary"), vmem_limit_bytes=VMEM_LIMIT),
        name="conv_merge",
    )(a, a, o, ga, gb, x, mod, *consts)


def _ffn_kernel(x_ref, mod_ref, wu_ref, fw_ref, fb_ref, wd_ref, g2_ref, b2_ref,
                out_ref, tail_ref, f_ref, *, d, alpha, hc, sub):
    i = pl.program_id(1)
    tm = x_ref.shape[1]
    pad = SUBLANES
    shift = mod_ref[0, :, 3 * d:4 * d]
    scale = mod_ref[0, :, 4 * d:5 * d]
    gate2 = mod_ref[0, :, 5 * d:6 * d]

    @pl.when(i == 0)
    def _():
        tail_ref[...] = jnp.zeros_like(tail_ref)

    def conv_chunk(u, c0, prev):
        cols = slice(c0, c0 + hc)
        h_pre = jnp.dot(u, wu_ref[:, cols], preferred_element_type=F32)
        w0, w1, w2 = (fw_ref[tap:tap + 1, cols] for tap in range(FFN_CONV_WIDTH))
        bias = fb_ref[:, cols]
        body = bias + w2 * h_pre + w1 * pltpu.roll(h_pre, 1, 0) + w0 * pltpu.roll(h_pre, 2, 0)
        top = jnp.concatenate([prev, h_pre[0:pad, :]], axis=0)
        head = (bias + w2 * top[pad:2 * pad, :] + w1 * top[pad - 1:2 * pad - 1, :]
                + w0 * top[pad - 2:2 * pad - 2, :])
        return jnp.concatenate([head, body[pad:, :]], axis=0), h_pre[sub - pad:sub, :]

    n_sub = tm // sub
    xs = [x_ref[0, s * sub:(s + 1) * sub, :] for s in range(n_sub)]
    us = [(_normalize(x) * (1.0 + scale) + shift).astype(BF16) for x in xs]
    for c in range(D_FF // hc):
        act_cols = slice(c * hc, (c + 1) * hc)
        lin_cols = slice(D_FF + c * hc, D_FF + (c + 1) * hc)
        prev_act, prev_lin = tail_ref[:, act_cols], tail_ref[:, lin_cols]
        for s in range(n_sub):
            h_act, prev_act = conv_chunk(us[s], c * hc, prev_act)
            h_lin, prev_lin = conv_chunk(us[s], D_FF + c * hc, prev_lin)
            gelu = 0.5 * h_act * (1.0 + lax.erf(h_act * (2.0 ** -0.5)))
            f_ref[s * sub:(s + 1) * sub, act_cols] = (gelu * h_lin).astype(BF16)
        tail_ref[:, act_cols] = prev_act
        tail_ref[:, lin_cols] = prev_lin

    for s in range(n_sub):
        rows = slice(s * sub, (s + 1) * sub)
        ffn = jnp.dot(f_ref[rows, :], wd_ref[...], preferred_element_type=F32)
        y = alpha * xs[s] + (1.0 + gate2) * ffn
        out_ref[0, rows, :] = _normalize(y) * g2_ref[...] + b2_ref[...]


def _ffn(x, mod, wu, fw, fb, wd, g2, b2, tm, alpha):
    bsz, seq, d = x.shape
    hc = 256
    assert D_FF % hc == 0 and FFN_CONV_WIDTH == 3
    grid = (bsz, seq // tm)
    row_spec = pl.BlockSpec((1, tm, d), lambda b, i: (b, i, 0))
    consts = (wu, fw, fb, wd, g2, b2)
    return pl.pallas_call(
        functools.partial(_ffn_kernel, d=d, alpha=alpha, hc=hc, sub=min(FFN_SUB_ROWS, tm)),
        grid=grid,
        in_specs=[row_spec, pl.BlockSpec((1, 1, mod.shape[-1]), lambda b, i: (b, 0, 0))]
                 + [_const_spec(t.shape) for t in consts],
        out_specs=row_spec,
        out_shape=jax.ShapeDtypeStruct((bsz, seq, d), F32),
        scratch_shapes=[pltpu.VMEM((SUBLANES, 2 * D_FF), F32),
                        pltpu.VMEM((tm, D_FF), BF16)],
        compiler_params=pltpu.CompilerParams(
            dimension_semantics=("arbitrary", "arbitrary"), vmem_limit_bytes=VMEM_LIMIT),
        name="conv_ffn",
    )(x, mod, *consts)


def _pack_inproj(w_in, b_in, d):
    g_off = 2 * CONV_CH + 3 * ATTN_W
    pad = LANES - ATTN_HEADS
    w_main = jnp.concatenate(
        [w_in[:, :g_off + ATTN_HEADS], jnp.zeros((d, pad), w_in.dtype)], axis=1).astype(BF16)
    b_main = jnp.concatenate([b_in[:g_off + ATTN_HEADS], jnp.zeros((pad,), b_in.dtype)])[None, :]
    w_gate = w_in[:, g_off + ATTN_HEADS:].astype(BF16)
    b_gate = b_in[g_off + ATTN_HEADS:][None, :]
    return w_main, b_main, w_gate, b_gate


def kernel(x, c, w_ada, b_ada, w_in, b_in, conv_a_w, conv_a_b, ln_conv_g, ln_conv_b,
           w_conv_proj, w_attn_proj, w_mix_out, b_mix_out, ln1_g, ln1_b,
           w_ffn_up, ffn_conv_w, ffn_conv_b, w_ffn_down, ln2_g, ln2_b):
    depth = w_ada.shape[0]
    bsz, seq, d = x.shape
    alpha = (2.0 * depth) ** 0.25
    tm = min(1024, seq)
    tq = min(512, seq)
    mod_all = _modulation(c, w_ada, b_ada)
    row = lambda t: t[None, :]
    for l in range(depth):
        mod = mod_all[l][:, None, :]
        w_main, b_main, w_gate, b_gate = _pack_inproj(w_in[l], b_in[l], d)
        a, q, k, v, ga, gb, aq, ak = _inproj(x, mod, w_main, b_main, w_gate, b_gate, tm)
        o = _attention(q, aq, k, v, ak, tq)
        x = _merge(a, o, ga, gb, x, mod, conv_a_w[l], row(conv_a_b[l]), row(ln_conv_g[l]),
                   row(ln_conv_b[l]), w_conv_proj[l].astype(BF16), w_attn_proj[l].astype(BF16),
                   w_mix_out[l].astype(BF16), row(b_mix_out[l]), row(ln1_g[l]), row(ln1_b[l]),
                   tm, alpha)
        x = _ffn(x, mod, w_ffn_up[l].astype(BF16), ffn_conv_w[l], row(ffn_conv_b[l]),
                 w_ffn_down[l].astype(BF16), row(ln2_g[l]), row(ln2_b[l]), tm, alpha)
    return x
```

```python
import functools
import math

import jax
import jax.numpy as jnp
from jax import lax
from jax.experimental import pallas as pl
from jax.experimental.pallas import tpu as pltpu

CONV_CH = 512
CONV_WIDTH = 31
ATTN_HEADS = 8
HEAD_DIM = 64
ATTN_W = ATTN_HEADS * HEAD_DIM
D_FF = 2816
FFN_CONV_WIDTH = 3
LN_EPS = 1e-5

LANES = 128
SUBLANES = 8
VMEM_LIMIT = 56 * 1024 * 1024

P_GLU = 0
P_Q = P_GLU + 2 * CONV_CH
P_K = P_Q + ATTN_W
P_V = P_K + ATTN_W
P_F = P_V + ATTN_W

F32 = jnp.float32
BF16 = jnp.bfloat16
NEG_BIG = -1e30
FFN_SUB_ROWS = 512
LOG2E = math.log2(math.e)
Q_SCALE = HEAD_DIM ** -0.5 * LOG2E


def _const_spec(shape):
    nd = len(shape)
    return pl.BlockSpec(shape, lambda *_: (0,) * nd, pipeline_mode=pl.Buffered(1))


def _normalize(x):
    mu = jnp.mean(x, axis=-1, keepdims=True)
    xc = x - mu
    var = jnp.mean(xc * xc, axis=-1, keepdims=True)
    return xc * lax.rsqrt(var + LN_EPS)


def _sigmoid(x):
    return jax.nn.sigmoid(x)


def _split3(x):
    hi = x.astype(BF16)
    r1 = x - hi.astype(F32)
    mid = r1.astype(BF16)
    lo = (r1 - mid.astype(F32)).astype(BF16)
    return hi, mid, lo


def _mod_kernel(c_ref, w_ref, b_ref, o_ref):
    c = c_ref[...]
    c_act = c * _sigmoid(c)
    o_ref[0] = jnp.dot(c_act, w_ref[0], preferred_element_type=F32,
                       precision=lax.Precision.HIGHEST) + b_ref[0]


def _modulation(c, w_ada, b_ada):
    depth, d, n = w_ada.shape
    bsz = c.shape[0]
    tn = n // 6
    return pl.pallas_call(
        _mod_kernel,
        grid=(depth, n // tn),
        in_specs=[
            pl.BlockSpec((bsz, d), lambda l, j: (0, 0)),
            pl.BlockSpec((1, d, tn), lambda l, j: (l, 0, j)),
            pl.BlockSpec((1, 1, tn), lambda l, j: (l, 0, j)),
        ],
        out_specs=pl.BlockSpec((1, bsz, tn), lambda l, j: (l, 0, j)),
        out_shape=jax.ShapeDtypeStruct((depth, bsz, n), F32),
        compiler_params=pltpu.CompilerParams(
            dimension_semantics=("arbitrary", "arbitrary"), vmem_limit_bytes=VMEM_LIMIT),
        name="adaln_modulation",
    )(c, w_ada, b_ada.reshape(depth, 1, n))


def _inproj_kernel(x_ref, mod_ref, w_ref, b_ref, wg_ref, bg_ref,
                   a_ref, q_ref, k_ref, v_ref, ga_ref, gb_ref, aq_ref, ak_ref, carry_ref, *, d):
    i = pl.program_id(1)
    tm = x_ref.shape[1]
    shift = mod_ref[0, :, 0:d]
    scale = mod_ref[0, :, d:2 * d]
    u = (_normalize(x_ref[0]) * (1.0 + scale) + shift).astype(BF16)

    def proj(lo, hi):
        return jnp.dot(u, w_ref[:, lo:hi], preferred_element_type=F32) + b_ref[:, lo:hi]

    val = proj(P_GLU, P_GLU + CONV_CH)
    gate = proj(P_GLU + CONV_CH, P_Q)
    a_ref[0] = (val * _sigmoid(gate)).astype(BF16)
    q_ref[0] = (proj(P_Q, P_K) * Q_SCALE).astype(BF16)
    k_ref[0] = proj(P_K, P_V).astype(BF16)
    v_ref[0] = proj(P_V, P_F).astype(BF16)
    ga_ref[0] = _sigmoid(jnp.dot(u, wg_ref[:, 0:d], preferred_element_type=F32)
                         + bg_ref[:, 0:d]).astype(BF16)
    gb_ref[0] = _sigmoid(jnp.dot(u, wg_ref[:, d:2 * d], preferred_element_type=F32)
                         + bg_ref[:, d:2 * d]).astype(BF16)

    zf = proj(P_F, P_F + LANES)
    log_f = jnp.minimum(zf, 0.0) - jnp.log1p(jnp.exp(-jnp.abs(zf)))

    @pl.when(i == 0)
    def _():
        carry_ref[...] = jnp.zeros_like(carry_ref)

    row = lax.broadcasted_iota(jnp.int32, (LANES, LANES), 0)
    col = lax.broadcasted_iota(jnp.int32, (LANES, LANES), 1)
    tri = jnp.where(row >= col, 1.0, 0.0).astype(BF16)
    group = col // ATTN_HEADS
    carry = carry_ref[0:1, :]
    for r in range(tm // LANES):
        rows = slice(r * LANES, (r + 1) * LANES)
        hi, mid, lo = _split3(log_f[rows, :])
        cs = (jnp.dot(tri, hi, preferred_element_type=F32)
              + jnp.dot(tri, mid, preferred_element_type=F32)
              + jnp.dot(tri, lo, preferred_element_type=F32)) + carry
        carry = cs[LANES - 1:LANES, :]
        parts = [p.astype(F32) for p in _split3(cs * LOG2E)]
        ak = jnp.where(group < 6, 1.0, 0.0)
        aq = jnp.where(group < 3, 1.0, 0.0)
        for g, part in enumerate(parts):
            placed_k = part if g == 0 else pltpu.roll(part, g * ATTN_HEADS, 1)
            ak = jnp.where(group == g, -placed_k, ak)
            aq = jnp.where(group == 3 + g, pltpu.roll(part, (3 + g) * ATTN_HEADS, 1), aq)
        ak_ref[0, rows, :] = ak.astype(BF16)
        aq_ref[0, rows, :] = aq.astype(BF16)
    carry_ref[0:1, :] = carry


def _inproj(x, mod, w_main, b_main, w_gate, b_gate, tm):
    bsz, seq, d = x.shape
    n_main = w_main.shape[1]
    grid = (bsz, seq // tm)
    row_spec = lambda w: pl.BlockSpec((1, tm, w), lambda b, i: (b, i, 0))
    out_shape = (
        jax.ShapeDtypeStruct((bsz, seq, CONV_CH), BF16),
        jax.ShapeDtypeStruct((bsz, seq, ATTN_W), BF16),
        jax.ShapeDtypeStruct((bsz, seq, ATTN_W), BF16),
        jax.ShapeDtypeStruct((bsz, seq, ATTN_W), BF16),
        jax.ShapeDtypeStruct((bsz, seq, d), BF16),
        jax.ShapeDtypeStruct((bsz, seq, d), BF16),
        jax.ShapeDtypeStruct((bsz, seq, LANES), BF16),
        jax.ShapeDtypeStruct((bsz, seq, LANES), BF16),
    )
    return pl.pallas_call(
        functools.partial(_inproj_kernel, d=d),
        grid=grid,
        in_specs=[
            row_spec(d),
            pl.BlockSpec((1, 1, mod.shape[-1]), lambda b, i: (b, 0, 0)),
            _const_spec((d, n_main)),
            _const_spec((1, n_main)),
            _const_spec((d, 2 * d)),
            _const_spec((1, 2 * d)),
        ],
        out_specs=(row_spec(CONV_CH), row_spec(ATTN_W), row_spec(ATTN_W), row_spec(ATTN_W),
                   row_spec(d), row_spec(d), row_spec(LANES), row_spec(LANES)),
        out_shape=out_shape,
        scratch_shapes=[pltpu.VMEM((SUBLANES, LANES), F32)],
        compiler_params=pltpu.CompilerParams(
            dimension_semantics=("arbitrary", "arbitrary"), vmem_limit_bytes=VMEM_LIMIT),
        name="inproj",
    )(x, mod, w_main, b_main, w_gate, b_gate)


def _attn_kernel(q_ref, aq_ref, k_ref, v_ref, ak_ref, o_ref, qa_ref, m_ref, acc_ref, *, tq):
    i = pl.program_id(1)
    qb = tq // 2
    lane = lax.broadcasted_iota(jnp.int32, (tq, LANES), 1)
    low_half = lane < HEAD_DIM
    own = (low_half, jnp.logical_not(low_half))
    zero = jnp.zeros((tq, LANES), BF16)

    aq = aq_ref[0]
    for h in range(ATTN_HEADS):
        pair, e = h // 2, h % 2
        q_own = jnp.where(own[e], q_ref[0, :, pair * LANES:(pair + 1) * LANES], zero)
        bias = jnp.where(lane % ATTN_HEADS == h, aq, zero)
        for sub in range(2):
            dst = slice((2 * sub + e) * qb, (2 * sub + e + 1) * qb)
            qa_ref[pair, dst, 0:LANES] = q_own[sub * qb:(sub + 1) * qb, :]
            qa_ref[pair, dst, LANES:2 * LANES] = bias[sub * qb:(sub + 1) * qb, :]

    def keys(rows, pair):
        blk = slice(pair * LANES, (pair + 1) * LANES)
        return jnp.concatenate([k_ref[0, rows, blk], ak_ref[0, rows, :]], axis=1)

    def values(rows, pair, e):
        vv = v_ref[0, rows, pair * LANES:(pair + 1) * LANES]
        low = lax.broadcasted_iota(jnp.int32, vv.shape, 1) < HEAD_DIM
        return jnp.where(low if e == 0 else jnp.logical_not(low), vv, jnp.ones_like(vv))

    def scores(lhs, kk):
        return lax.dot_general(lhs, kk, (((1,), (1,)), ((), ())),
                               preferred_element_type=F32)

    def probs(s, m):
        return jnp.exp2(jnp.concatenate(
            [s[:, c * LANES:(c + 1) * LANES] - m for c in range(s.shape[1] // LANES)],
            axis=1).astype(BF16))

    def row_max(s):
        return jnp.broadcast_to(jnp.max(s, axis=-1, keepdims=True), (s.shape[0], LANES))

    def diagonal_block(q0):
        rows0, rows1 = pl.ds(q0, qb), pl.ds(q0 + qb, qb)
        row = lax.broadcasted_iota(jnp.int32, (qb, qb), 0)
        col = lax.broadcasted_iota(jnp.int32, (qb, qb), 1)
        causal = row >= col
        for pair in range(ATTN_HEADS // 2):
            d0 = scores(qa_ref[pair], keys(rows0, pair))
            d1 = scores(qa_ref[pair, 2 * qb:4 * qb, :], keys(rows1, pair))
            for e in range(2):
                h = 2 * pair + e
                s00 = jnp.where(causal, d0[e * qb:(e + 1) * qb, :], NEG_BIG)
                s10 = d0[(2 + e) * qb:(3 + e) * qb, :]
                s11 = jnp.where(causal, d1[e * qb:(e + 1) * qb, :], NEG_BIG)
                m0 = row_max(s00)
                m1 = jnp.maximum(row_max(s10), row_max(s11))
                p_first = jnp.concatenate([probs(s00, m0), probs(s10, m1)], axis=0)
                pv_first = jnp.dot(p_first, values(rows0, pair, e), preferred_element_type=F32)
                pv_second = jnp.dot(probs(s11, m1), values(rows1, pair, e),
                                    preferred_element_type=F32)
                acc_ref[h, 0:qb, :] = pv_first[0:qb, :]
                acc_ref[h, qb:tq, :] = pv_first[qb:tq, :] + pv_second
                m_ref[h, 0:qb, :] = m0
                m_ref[h, qb:tq, :] = m1

    def full_block(j, carry):
        rows = pl.ds(pl.multiple_of(j * tq, tq), tq)
        for pair in range(ATTN_HEADS // 2):
            s_pair = scores(qa_ref[pair], keys(rows, pair))
            for e in range(2):
                h = 2 * pair + e
                s = jnp.concatenate([s_pair[e * qb:(e + 1) * qb, :],
                                     s_pair[(2 + e) * qb:(3 + e) * qb, :]], axis=0)
                m_prev = m_ref[h]
                m_new = jnp.maximum(m_prev, row_max(s))
                pv = jnp.dot(probs(s, m_new), values(rows, pair, e), preferred_element_type=F32)
                acc_ref[h] = jnp.exp2(m_prev - m_new) * acc_ref[h] + pv
                m_ref[h] = m_new
        return carry

    diagonal_block(pl.multiple_of(i * tq, tq))
    lax.fori_loop(0, i, full_block, 0)

    for pair in range(ATTN_HEADS // 2):
        acc0, acc1 = acc_ref[2 * pair], acc_ref[2 * pair + 1]
        numer = jnp.where(low_half, acc0, acc1)
        denom = pltpu.roll(jnp.where(low_half, acc1, acc0), HEAD_DIM, 1)
        o_ref[0, :, pair * LANES:(pair + 1) * LANES] = (numer / denom).astype(o_ref.dtype)


def _attention(q, aq, k, v, ak, tq):
    bsz, seq, _ = q.shape
    grid = (bsz, seq // tq)
    q_spec = lambda w: pl.BlockSpec((1, tq, w), lambda b, i: (b, i, 0))
    kv_spec = lambda w: pl.BlockSpec((1, seq, w), lambda b, i: (b, 0, 0))
    return pl.pallas_call(
        functools.partial(_attn_kernel, tq=tq),
        grid=grid,
        in_specs=[q_spec(ATTN_W), q_spec(LANES), kv_spec(ATTN_W), kv_spec(ATTN_W), kv_spec(LANES)],
        out_specs=q_spec(ATTN_W),
        out_shape=jax.ShapeDtypeStruct((bsz, seq, ATTN_W), BF16),
        scratch_shapes=[pltpu.VMEM((ATTN_HEADS // 2, 2 * tq, 2 * LANES), BF16),
                        pltpu.VMEM((ATTN_HEADS, tq, LANES), F32),
                        pltpu.VMEM((ATTN_HEADS, tq, LANES), F32)],
        compiler_params=pltpu.CompilerParams(
            dimension_semantics=("arbitrary", "arbitrary"), vmem_limit_bytes=VMEM_LIMIT),
        name="fox_attention",
    )(q, aq, k, v, ak)


def _merge_kernel(a_ref, halo_ref, o_ref, ga_ref, gb_ref, x_ref, mod_ref,
                  cw_ref, cb_ref, lcg_ref, lcb_ref, wc_ref, wa_ref, wm_ref, bm_ref,
                  g1_ref, b1_ref, out_ref, shift_ref, conv_ref, *, d, alpha, halo):
    i = pl.program_id(1)
    tm = a_ref.shape[1]
    ext_rows = halo + tm
    first = halo - (CONV_WIDTH - 1)
    row_block = 128

    for lc in range(CONV_CH // LANES):
        lanes = slice(lc * LANES, (lc + 1) * LANES)
        left = jnp.where(i > 0, halo_ref[0, :, lanes].astype(F32), 0.0)
        ext = jnp.concatenate([left, a_ref[0, :, lanes].astype(F32)], axis=0)
        for r in range(SUBLANES):
            shift_ref[r] = ext if r == 0 else pltpu.roll(ext, ext_rows - r, 0)
        for rb in range(tm // row_block):
            acc = jnp.zeros((row_block, LANES), F32) + cb_ref[:, lanes]
            for tap in range(CONV_WIDTH):
                off = first + tap
                base = (off // SUBLANES) * SUBLANES + rb * row_block
                acc = acc + cw_ref[tap:tap + 1, lanes] * shift_ref[off % SUBLANES,
                                                                   base:base + row_block, :]
            conv_ref[rb * row_block:(rb + 1) * row_block, lanes] = acc
    conv = conv_ref[...]

    act = _normalize(conv) * lcg_ref[...] + lcb_ref[...]
    act = (act * _sigmoid(act)).astype(BF16)
    y_a = jnp.dot(act, wc_ref[...], preferred_element_type=F32)
    y_b = jnp.dot(o_ref[0], wa_ref[...], preferred_element_type=F32)
    merged = (ga_ref[0].astype(F32) * y_a + gb_ref[0].astype(F32) * y_b).astype(BF16)
    mix = jnp.dot(merged, wm_ref[...], preferred_element_type=F32) + bm_ref[...]
    gate1 = mod_ref[0, :, 2 * d:3 * d]
    y = alpha * x_ref[0] + (1.0 + gate1) * mix
    out_ref[0] = _normalize(y) * g1_ref[...] + b1_ref[...]


def _merge(a, o, ga, gb, x, mod, cw, cb, lcg, lcb, wc, wa, wm, bm, g1, b1, tm, alpha):
    bsz, seq, d = x.shape
    halo = 32
    assert halo >= CONV_WIDTH - 1 and tm % halo == 0
    grid = (bsz, seq // tm)
    row_spec = lambda w: pl.BlockSpec((1, tm, w), lambda b, i: (b, i, 0))
    halo_spec = pl.BlockSpec(
        (1, halo, CONV_CH), lambda b, i: (b, jnp.maximum(i * (tm // halo) - 1, 0), 0))
    consts = (cw, cb, lcg, lcb, wc, wa, wm, bm, g1, b1)
    return pl.pallas_call(
        functools.partial(_merge_kernel, d=d, alpha=alpha, halo=halo),
        grid=grid,
        in_specs=[row_spec(CONV_CH), halo_spec, row_spec(ATTN_W), row_spec(d), row_spec(d),
                  row_spec(d), pl.BlockSpec((1, 1, mod.shape[-1]), lambda b, i: (b, 0, 0))]
                 + [_const_spec(t.shape) for t in consts],
        out_specs=row_spec(d),
        out_shape=jax.ShapeDtypeStruct((bsz, seq, d), F32),
        scratch_shapes=[pltpu.VMEM((SUBLANES, halo + tm, LANES), F32),
                        pltpu.VMEM((tm, CONV_CH), F32)],
        compiler_params=pltpu.CompilerParams(
            dimension_semantics=("arbitrary", "arbitrary"), vmem_limit_bytes=VMEM_LIMIT),
        name="conv_merge",
    )(a, a, o, ga, gb, x, mod, *consts)


def _ffn_kernel(x_ref, mod_ref, wu_ref, fw_ref, fb_ref, wd_ref, g2_ref, b2_ref,
                out_ref, tail_ref, f_ref, *, d, alpha, hc, sub):
    i = pl.program_id(1)
    tm = x_ref.shape[1]
    pad = SUBLANES
    shift = mod_ref[0, :, 3 * d:4 * d]
    scale = mod_ref[0, :, 4 * d:5 * d]
    gate2 = mod_ref[0, :, 5 * d:6 * d]

    @pl.when(i == 0)
    def _():
        tail_ref[...] = jnp.zeros_like(tail_ref)

    def conv_chunk(u, c0, prev):
        cols = slice(c0, c0 + hc)
        h_pre = jnp.dot(u, wu_ref[:, cols], preferred_element_type=F32)
        w0, w1, w2 = (fw_ref[tap:tap + 1, cols] for tap in range(FFN_CONV_WIDTH))
        bias = fb_ref[:, cols]
        body = bias + w2 * h_pre + w1 * pltpu.roll(h_pre, 1, 0) + w0 * pltpu.roll(h_pre, 2, 0)
        top = jnp.concatenate([prev, h_pre[0:pad, :]], axis=0)
        head = (bias + w2 * top[pad:2 * pad, :] + w1 * top[pad - 1:2 * pad - 1, :]
                + w0 * top[pad - 2:2 * pad - 2, :])
        return jnp.concatenate([head, body[pad:, :]], axis=0), h_pre[sub - pad:sub, :]

    n_sub = tm // sub
    xs = [x_ref[0, s * sub:(s + 1) * sub, :] for s in range(n_sub)]
    us = [(_normalize(x) * (1.0 + scale) + shift).astype(BF16) for x in xs]
    for c in range(D_FF // hc):
        act_cols = slice(c * hc, (c + 1) * hc)
        lin_cols = slice(D_FF + c * hc, D_FF + (c + 1) * hc)
        prev_act, prev_lin = tail_ref[:, act_cols], tail_ref[:, lin_cols]
        for s in range(n_sub):
            h_act, prev_act = conv_chunk(us[s], c * hc, prev_act)
            h_lin, prev_lin = conv_chunk(us[s], D_FF + c * hc, prev_lin)
            gelu2 = h_act * (1.0 + lax.erf(h_act * (2.0 ** -0.5)))
            f_ref[s * sub:(s + 1) * sub, act_cols] = (gelu2 * h_lin).astype(BF16)
        tail_ref[:, act_cols] = prev_act
        tail_ref[:, lin_cols] = prev_lin

    for s in range(n_sub):
        rows = slice(s * sub, (s + 1) * sub)
        ffn = jnp.dot(f_ref[rows, :], wd_ref[...], preferred_element_type=F32)
        y = alpha * xs[s] + (1.0 + gate2) * ffn
        out_ref[0, rows, :] = _normalize(y) * g2_ref[...] + b2_ref[...]


def _ffn(x, mod, wu, fw, fb, wd, g2, b2, tm, alpha):
    bsz, seq, d = x.shape
    hc = 256
    assert D_FF % hc == 0 and FFN_CONV_WIDTH == 3
    grid = (bsz, seq // tm)
    row_spec = pl.BlockSpec((1, tm, d), lambda b, i: (b, i, 0))
    consts = (wu, fw, fb, wd, g2, b2)
    return pl.pallas_call(
        functools.partial(_ffn_kernel, d=d, alpha=alpha, hc=hc, sub=min(FFN_SUB_ROWS, tm)),
        grid=grid,
        in_specs=[row_spec, pl.BlockSpec((1, 1, mod.shape[-1]), lambda b, i: (b, 0, 0))]
                 + [_const_spec(t.shape) for t in consts],
        out_specs=row_spec,
        out_shape=jax.ShapeDtypeStruct((bsz, seq, d), F32),
        scratch_shapes=[pltpu.VMEM((SUBLANES, 2 * D_FF), F32),
                        pltpu.VMEM((tm, D_FF), BF16)],
        compiler_params=pltpu.CompilerParams(
            dimension_semantics=("arbitrary", "arbitrary"), vmem_limit_bytes=VMEM_LIMIT),
        name="conv_ffn",
    )(x, mod, *consts)


def _pack_inproj(w_in, b_in, d):
    g_off = 2 * CONV_CH + 3 * ATTN_W
    pad = LANES - ATTN_HEADS
    w_main = jnp.concatenate(
        [w_in[:, :g_off + ATTN_HEADS], jnp.zeros((d, pad), w_in.dtype)], axis=1).astype(BF16)
    b_main = jnp.concatenate([b_in[:g_off + ATTN_HEADS], jnp.zeros((pad,), b_in.dtype)])[None, :]
    w_gate = w_in[:, g_off + ATTN_HEADS:].astype(BF16)
    b_gate = b_in[g_off + ATTN_HEADS:][None, :]
    return w_main, b_main, w_gate, b_gate


def kernel(x, c, w_ada, b_ada, w_in, b_in, conv_a_w, conv_a_b, ln_conv_g, ln_conv_b,
           w_conv_proj, w_attn_proj, w_mix_out, b_mix_out, ln1_g, ln1_b,
           w_ffn_up, ffn_conv_w, ffn_conv_b, w_ffn_down, ln2_g, ln2_b):
    depth = w_ada.shape[0]
    bsz, seq, d = x.shape
    alpha = (2.0 * depth) ** 0.25
    tm = min(1024, seq)
    tq = min(512, seq)
    mod_all = _modulation(c, w_ada, b_ada)
    row = lambda t: t[None, :]
    for l in range(depth):
        mod = mod_all[l][:, None, :]
        w_main, b_main, w_gate, b_gate = _pack_inproj(w_in[l], b_in[l], d)
        a, q, k, v, ga, gb, aq, ak = _inproj(x, mod, w_main, b_main, w_gate, b_gate, tm)
        o = _attention(q, aq, k, v, ak, tq)
        x = _merge(a, o, ga, gb, x, mod, conv_a_w[l], row(conv_a_b[l]), row(ln_conv_g[l]),
                   row(ln_conv_b[l]), w_conv_proj[l].astype(BF16), w_attn_proj[l].astype(BF16),
                   w_mix_out[l].astype(BF16), row(b_mix_out[l]), row(ln1_g[l]), row(ln1_b[l]),
                   tm, alpha)
        x = _ffn(x, mod, w_ffn_up[l].astype(BF16), ffn_conv_w[l], row(ffn_conv_b[l]),
                 (0.5 * w_ffn_down[l]).astype(BF16), row(ln2_g[l]), row(ln2_b[l]), tm, alpha)
    return x
```

```python
import functools
import math

import jax
import jax.numpy as jnp
from jax import lax
from jax.experimental import pallas as pl
from jax.experimental.pallas import tpu as pltpu

CONV_CH = 512
CONV_WIDTH = 31
ATTN_HEADS = 8
HEAD_DIM = 64
ATTN_W = ATTN_HEADS * HEAD_DIM
D_FF = 2816
FFN_CONV_WIDTH = 3
LN_EPS = 1e-5

LANES = 128
SUBLANES = 8
VMEM_LIMIT = 56 * 1024 * 1024

P_GLU = 0
P_Q = P_GLU + 2 * CONV_CH
P_K = P_Q + ATTN_W
P_V = P_K + ATTN_W
P_F = P_V + ATTN_W

F32 = jnp.float32
BF16 = jnp.bfloat16
NEG_BIG = -1e30
SUB_ROWS = 512
LOG2E = math.log2(math.e)
Q_SCALE = HEAD_DIM ** -0.5 * LOG2E


def _const_spec(shape):
    nd = len(shape)
    return pl.BlockSpec(shape, lambda *_: (0,) * nd, pipeline_mode=pl.Buffered(1))


def _normalize(x):
    mu = jnp.mean(x, axis=-1, keepdims=True)
    xc = x - mu
    var = jnp.mean(xc * xc, axis=-1, keepdims=True)
    return xc * lax.rsqrt(var + LN_EPS)


def _sigmoid(x):
    return jax.nn.sigmoid(x)


def _split3(x):
    hi = x.astype(BF16)
    r1 = x - hi.astype(F32)
    mid = r1.astype(BF16)
    lo = (r1 - mid.astype(F32)).astype(BF16)
    return hi, mid, lo


def _mod_kernel(c_ref, w_ref, b_ref, o_ref):
    c = c_ref[...]
    c_act = c * _sigmoid(c)
    o_ref[0] = jnp.dot(c_act, w_ref[0], preferred_element_type=F32,
                       precision=lax.Precision.HIGHEST) + b_ref[0]


def _modulation(c, w_ada, b_ada):
    depth, d, n = w_ada.shape
    bsz = c.shape[0]
    tn = n // 6
    return pl.pallas_call(
        _mod_kernel,
        grid=(depth, n // tn),
        in_specs=[
            pl.BlockSpec((bsz, d), lambda l, j: (0, 0)),
            pl.BlockSpec((1, d, tn), lambda l, j: (l, 0, j)),
            pl.BlockSpec((1, 1, tn), lambda l, j: (l, 0, j)),
        ],
        out_specs=pl.BlockSpec((1, bsz, tn), lambda l, j: (l, 0, j)),
        out_shape=jax.ShapeDtypeStruct((depth, bsz, n), F32),
        compiler_params=pltpu.CompilerParams(
            dimension_semantics=("arbitrary", "arbitrary"), vmem_limit_bytes=VMEM_LIMIT),
        name="adaln_modulation",
    )(c, w_ada, b_ada.reshape(depth, 1, n))


def _inproj_kernel(x_ref, mod_ref, w_ref, b_ref, wg_ref, bg_ref,
                   a_ref, q_ref, k_ref, v_ref, ga_ref, gb_ref, aq_ref, ak_ref, carry_ref, *, d):
    i = pl.program_id(1)
    tm = x_ref.shape[1]
    shift = mod_ref[0, :, 0:d]
    scale = mod_ref[0, :, d:2 * d]
    u = (_normalize(x_ref[0]) * (1.0 + scale) + shift).astype(BF16)

    def proj(lo, hi):
        return jnp.dot(u, w_ref[:, lo:hi], preferred_element_type=F32) + b_ref[:, lo:hi]

    val = proj(P_GLU, P_GLU + CONV_CH)
    gate = proj(P_GLU + CONV_CH, P_Q)
    a_ref[0] = (val * _sigmoid(gate)).astype(BF16)
    q_ref[0] = (proj(P_Q, P_K) * Q_SCALE).astype(BF16)
    k_ref[0] = proj(P_K, P_V).astype(BF16)
    v_ref[0] = proj(P_V, P_F).astype(BF16)
    ga_ref[0] = _sigmoid(jnp.dot(u, wg_ref[:, 0:d], preferred_element_type=F32)
                         + bg_ref[:, 0:d]).astype(BF16)
    gb_ref[0] = _sigmoid(jnp.dot(u, wg_ref[:, d:2 * d], preferred_element_type=F32)
                         + bg_ref[:, d:2 * d]).astype(BF16)

    zf = proj(P_F, P_F + LANES)
    log_f = jnp.minimum(zf, 0.0) - jnp.log1p(jnp.exp(-jnp.abs(zf)))

    @pl.when(i == 0)
    def _():
        carry_ref[...] = jnp.zeros_like(carry_ref)

    row = lax.broadcasted_iota(jnp.int32, (LANES, LANES), 0)
    col = lax.broadcasted_iota(jnp.int32, (LANES, LANES), 1)
    tri = jnp.where(row >= col, 1.0, 0.0).astype(BF16)
    group = col // ATTN_HEADS
    carry = carry_ref[0:1, :]
    for r in range(tm // LANES):
        rows = slice(r * LANES, (r + 1) * LANES)
        hi, mid, lo = _split3(log_f[rows, :])
        cs = (jnp.dot(tri, hi, preferred_element_type=F32)
              + jnp.dot(tri, mid, preferred_element_type=F32)
              + jnp.dot(tri, lo, preferred_element_type=F32)) + carry
        carry = cs[LANES - 1:LANES, :]
        parts = [p.astype(F32) for p in _split3(cs * LOG2E)]
        ak = jnp.where(group < 6, 1.0, 0.0)
        aq = jnp.where(group < 3, 1.0, 0.0)
        for g, part in enumerate(parts):
            placed_k = part if g == 0 else pltpu.roll(part, g * ATTN_HEADS, 1)
            ak = jnp.where(group == g, -placed_k, ak)
            aq = jnp.where(group == 3 + g, pltpu.roll(part, (3 + g) * ATTN_HEADS, 1), aq)
        ak_ref[0, rows, :] = ak.astype(BF16)
        aq_ref[0, rows, :] = aq.astype(BF16)
    carry_ref[0:1, :] = carry


def _inproj(x, mod, w_main, b_main, w_gate, b_gate, tm):
    bsz, seq, d = x.shape
    n_main = w_main.shape[1]
    grid = (bsz, seq // tm)
    row_spec = lambda w: pl.BlockSpec((1, tm, w), lambda b, i: (b, i, 0))
    out_shape = (
        jax.ShapeDtypeStruct((bsz, seq, CONV_CH), BF16),
        jax.ShapeDtypeStruct((bsz, seq, ATTN_W), BF16),
        jax.ShapeDtypeStruct((bsz, seq, ATTN_W), BF16),
        jax.ShapeDtypeStruct((bsz, seq, ATTN_W), BF16),
        jax.ShapeDtypeStruct((bsz, seq, d), BF16),
        jax.ShapeDtypeStruct((bsz, seq, d), BF16),
        jax.ShapeDtypeStruct((bsz, seq, LANES), BF16),
        jax.ShapeDtypeStruct((bsz, seq, LANES), BF16),
    )
    return pl.pallas_call(
        functools.partial(_inproj_kernel, d=d),
        grid=grid,
        in_specs=[
            row_spec(d),
            pl.BlockSpec((1, 1, mod.shape[-1]), lambda b, i: (b, 0, 0)),
            _const_spec((d, n_main)),
            _const_spec((1, n_main)),
            _const_spec((d, 2 * d)),
            _const_spec((1, 2 * d)),
        ],
        out_specs=(row_spec(CONV_CH), row_spec(ATTN_W), row_spec(ATTN_W), row_spec(ATTN_W),
                   row_spec(d), row_spec(d), row_spec(LANES), row_spec(LANES)),
        out_shape=out_shape,
        scratch_shapes=[pltpu.VMEM((SUBLANES, LANES), F32)],
        compiler_params=pltpu.CompilerParams(
            dimension_semantics=("arbitrary", "arbitrary"), vmem_limit_bytes=VMEM_LIMIT),
        name="inproj",
    )(x, mod, w_main, b_main, w_gate, b_gate)


def _attn_kernel(q_ref, aq_ref, k_ref, v_ref, ak_ref, o_ref, qa_ref, m_ref, acc_ref, *, tq):
    i = pl.program_id(1)
    qb = tq // 2
    lane = lax.broadcasted_iota(jnp.int32, (tq, LANES), 1)
    low_half = lane < HEAD_DIM
    own = (low_half, jnp.logical_not(low_half))
    zero = jnp.zeros((tq, LANES), BF16)

    aq = aq_ref[0]
    for h in range(ATTN_HEADS):
        pair, e = h // 2, h % 2
        q_own = jnp.where(own[e], q_ref[0, :, pair * LANES:(pair + 1) * LANES], zero)
        bias = jnp.where(lane % ATTN_HEADS == h, aq, zero)
        for sub in range(2):
            dst = slice((2 * sub + e) * qb, (2 * sub + e + 1) * qb)
            qa_ref[pair, dst, 0:LANES] = q_own[sub * qb:(sub + 1) * qb, :]
            qa_ref[pair, dst, LANES:2 * LANES] = bias[sub * qb:(sub + 1) * qb, :]

    def keys(rows, pair):
        blk = slice(pair * LANES, (pair + 1) * LANES)
        return jnp.concatenate([k_ref[0, rows, blk], ak_ref[0, rows, :]], axis=1)

    def values(rows, pair, e):
        vv = v_ref[0, rows, pair * LANES:(pair + 1) * LANES]
        low = lax.broadcasted_iota(jnp.int32, vv.shape, 1) < HEAD_DIM
        return jnp.where(low if e == 0 else jnp.logical_not(low), vv, jnp.ones_like(vv))

    def scores(lhs, kk):
        return lax.dot_general(lhs, kk, (((1,), (1,)), ((), ())),
                               preferred_element_type=F32)

    def probs(s, m):
        return jnp.exp2(jnp.concatenate(
            [s[:, c * LANES:(c + 1) * LANES] - m for c in range(s.shape[1] // LANES)],
            axis=1).astype(BF16))

    def row_max(s):
        return jnp.broadcast_to(jnp.max(s, axis=-1, keepdims=True), (s.shape[0], LANES))

    def diagonal_block(q0):
        rows0, rows1 = pl.ds(q0, qb), pl.ds(q0 + qb, qb)
        row = lax.broadcasted_iota(jnp.int32, (qb, qb), 0)
        col = lax.broadcasted_iota(jnp.int32, (qb, qb), 1)
        causal = row >= col
        for pair in range(ATTN_HEADS // 2):
            d0 = scores(qa_ref[pair], keys(rows0, pair))
            d1 = scores(qa_ref[pair, 2 * qb:4 * qb, :], keys(rows1, pair))
            for e in range(2):
                h = 2 * pair + e
                s00 = jnp.where(causal, d0[e * qb:(e + 1) * qb, :], NEG_BIG)
                s10 = d0[(2 + e) * qb:(3 + e) * qb, :]
                s11 = jnp.where(causal, d1[e * qb:(e + 1) * qb, :], NEG_BIG)
                m0 = row_max(s00)
                m1 = jnp.maximum(row_max(s10), row_max(s11))
                p_first = jnp.concatenate([probs(s00, m0), probs(s10, m1)], axis=0)
                pv_first = jnp.dot(p_first, values(rows0, pair, e), preferred_element_type=F32)
                pv_second = jnp.dot(probs(s11, m1), values(rows1, pair, e),
                                    preferred_element_type=F32)
                acc_ref[h, 0:qb, :] = pv_first[0:qb, :]
                acc_ref[h, qb:tq, :] = pv_first[qb:tq, :] + pv_second
                m_ref[h, 0:qb, :] = m0
                m_ref[h, qb:tq, :] = m1

    def full_block(j, carry):
        rows = pl.ds(pl.multiple_of(j * tq, tq), tq)
        for pair in range(ATTN_HEADS // 2):
            s_pair = scores(qa_ref[pair], keys(rows, pair))
            for e in range(2):
                h = 2 * pair + e
                s = jnp.concatenate([s_pair[e * qb:(e + 1) * qb, :],
                                     s_pair[(2 + e) * qb:(3 + e) * qb, :]], axis=0)
                m_prev = m_ref[h]
                m_new = jnp.maximum(m_prev, row_max(s))
                pv = jnp.dot(probs(s, m_new), values(rows, pair, e), preferred_element_type=F32)
                acc_ref[h] = jnp.exp2(m_prev - m_new) * acc_ref[h] + pv
                m_ref[h] = m_new
        return carry

    diagonal_block(pl.multiple_of(i * tq, tq))
    lax.fori_loop(0, i, full_block, 0)

    for pair in range(ATTN_HEADS // 2):
        acc0, acc1 = acc_ref[2 * pair], acc_ref[2 * pair + 1]
        numer = jnp.where(low_half, acc0, acc1)
        denom = pltpu.roll(jnp.where(low_half, acc1, acc0), HEAD_DIM, 1)
        o_ref[0, :, pair * LANES:(pair + 1) * LANES] = (numer / denom).astype(o_ref.dtype)


def _attention(q, aq, k, v, ak, tq):
    bsz, seq, _ = q.shape
    grid = (bsz, seq // tq)
    q_spec = lambda w: pl.BlockSpec((1, tq, w), lambda b, i: (b, i, 0))
    kv_spec = lambda w: pl.BlockSpec((1, seq, w), lambda b, i: (b, 0, 0))
    return pl.pallas_call(
        functools.partial(_attn_kernel, tq=tq),
        grid=grid,
        in_specs=[q_spec(ATTN_W), q_spec(LANES), kv_spec(ATTN_W), kv_spec(ATTN_W), kv_spec(LANES)],
        out_specs=q_spec(ATTN_W),
        out_shape=jax.ShapeDtypeStruct((bsz, seq, ATTN_W), BF16),
        scratch_shapes=[pltpu.VMEM((ATTN_HEADS // 2, 2 * tq, 2 * LANES), BF16),
                        pltpu.VMEM((ATTN_HEADS, tq, LANES), F32),
                        pltpu.VMEM((ATTN_HEADS, tq, LANES), F32)],
        compiler_params=pltpu.CompilerParams(
            dimension_semantics=("arbitrary", "arbitrary"), vmem_limit_bytes=VMEM_LIMIT),
        name="fox_attention",
    )(q, aq, k, v, ak)


def _merge_kernel(a_ref, halo_ref, o_ref, ga_ref, gb_ref, x_ref, mod_ref,
                  cw_ref, cb_ref, lcg_ref, lcb_ref, wc_ref, wa_ref, wm_ref, bm_ref,
                  g1_ref, b1_ref, out_ref, shift_ref, conv_ref, *, d, alpha, halo):
    i = pl.program_id(1)
    tm = a_ref.shape[1]
    ext_rows = halo + tm
    first = halo - (CONV_WIDTH - 1)
    row_block = 128

    for lc in range(CONV_CH // LANES):
        lanes = slice(lc * LANES, (lc + 1) * LANES)
        left = jnp.where(i > 0, halo_ref[0, :, lanes].astype(F32), 0.0)
        ext = jnp.concatenate([left, a_ref[0, :, lanes].astype(F32)], axis=0)
        for r in range(SUBLANES):
            shift_ref[r] = ext if r == 0 else pltpu.roll(ext, ext_rows - r, 0)
        for rb in range(tm // row_block):
            acc = jnp.zeros((row_block, LANES), F32) + cb_ref[:, lanes]
            for tap in range(CONV_WIDTH):
                off = first + tap
                base = (off // SUBLANES) * SUBLANES + rb * row_block
                acc = acc + cw_ref[tap:tap + 1, lanes] * shift_ref[off % SUBLANES,
                                                                   base:base + row_block, :]
            conv_ref[rb * row_block:(rb + 1) * row_block, lanes] = acc
    conv = conv_ref[...]

    gate1 = mod_ref[0, :, 2 * d:3 * d]
    sub = min(SUB_ROWS, tm)
    for s in range(tm // sub):
        rows = slice(s * sub, (s + 1) * sub)
        act = _normalize(conv[rows, :]) * lcg_ref[...] + lcb_ref[...]
        act = (act * _sigmoid(act)).astype(BF16)
        y_a = jnp.dot(act, wc_ref[...], preferred_element_type=F32)
        y_b = jnp.dot(o_ref[0, rows, :], wa_ref[...], preferred_element_type=F32)
        merged = (ga_ref[0, rows, :].astype(F32) * y_a
                  + gb_ref[0, rows, :].astype(F32) * y_b).astype(BF16)
        mix = jnp.dot(merged, wm_ref[...], preferred_element_type=F32) + bm_ref[...]
        y = alpha * x_ref[0, rows, :] + (1.0 + gate1) * mix
        out_ref[0, rows, :] = _normalize(y) * g1_ref[...] + b1_ref[...]


def _merge(a, o, ga, gb, x, mod, cw, cb, lcg, lcb, wc, wa, wm, bm, g1, b1, tm, alpha):
    bsz, seq, d = x.shape
    halo = 32
    assert halo >= CONV_WIDTH - 1 and tm % halo == 0
    grid = (bsz, seq // tm)
    row_spec = lambda w: pl.BlockSpec((1, tm, w), lambda b, i: (b, i, 0))
    halo_spec = pl.BlockSpec(
        (1, halo, CONV_CH), lambda b, i: (b, jnp.maximum(i * (tm // halo) - 1, 0), 0))
    consts = (cw, cb, lcg, lcb, wc, wa, wm, bm, g1, b1)
    return pl.pallas_call(
        functools.partial(_merge_kernel, d=d, alpha=alpha, halo=halo),
        grid=grid,
        in_specs=[row_spec(CONV_CH), halo_spec, row_spec(ATTN_W), row_spec(d), row_spec(d),
                  row_spec(d), pl.BlockSpec((1, 1, mod.shape[-1]), lambda b, i: (b, 0, 0))]
                 + [_const_spec(t.shape) for t in consts],
        out_specs=row_spec(d),
        out_shape=jax.ShapeDtypeStruct((bsz, seq, d), F32),
        scratch_shapes=[pltpu.VMEM((SUBLANES, halo + tm, LANES), F32),
                        pltpu.VMEM((tm, CONV_CH), F32)],
        compiler_params=pltpu.CompilerParams(
            dimension_semantics=("arbitrary", "arbitrary"), vmem_limit_bytes=VMEM_LIMIT),
        name="conv_merge",
    )(a, a, o, ga, gb, x, mod, *consts)


def _ffn_kernel(x_ref, mod_ref, wu_ref, fw_ref, fb_ref, wd_ref, g2_ref, b2_ref,
                out_ref, tail_ref, f_ref, *, d, alpha, hc, sub):
    i = pl.program_id(1)
    tm = x_ref.shape[1]
    pad = SUBLANES
    shift = mod_ref[0, :, 3 * d:4 * d]
    scale = mod_ref[0, :, 4 * d:5 * d]
    gate2 = mod_ref[0, :, 5 * d:6 * d]

    @pl.when(i == 0)
    def _():
        tail_ref[...] = jnp.zeros_like(tail_ref)

    def conv_chunk(u, c0, prev):
        cols = slice(c0, c0 + hc)
        h_pre = jnp.dot(u, wu_ref[:, cols], preferred_element_type=F32)
        w0, w1, w2 = (fw_ref[tap:tap + 1, cols] for tap in range(FFN_CONV_WIDTH))
        bias = fb_ref[:, cols]
        body = bias + w2 * h_pre + w1 * pltpu.roll(h_pre, 1, 0) + w0 * pltpu.roll(h_pre, 2, 0)
        top = jnp.concatenate([prev, h_pre[0:pad, :]], axis=0)
        head = (bias + w2 * top[pad:2 * pad, :] + w1 * top[pad - 1:2 * pad - 1, :]
                + w0 * top[pad - 2:2 * pad - 2, :])
        return jnp.concatenate([head, body[pad:, :]], axis=0), h_pre[sub - pad:sub, :]

    n_sub = tm // sub
    xs = [x_ref[0, s * sub:(s + 1) * sub, :] for s in range(n_sub)]
    us = [(_normalize(x) * (1.0 + scale) + shift).astype(BF16) for x in xs]
    for c in range(D_FF // hc):
        act_cols = slice(c * hc, (c + 1) * hc)
        lin_cols = slice(D_FF + c * hc, D_FF + (c + 1) * hc)
        prev_act, prev_lin = tail_ref[:, act_cols], tail_ref[:, lin_cols]
        for s in range(n_sub):
            h_act, prev_act = conv_chunk(us[s], c * hc, prev_act)
            h_lin, prev_lin = conv_chunk(us[s], D_FF + c * hc, prev_lin)
            gelu2 = h_act * (1.0 + lax.erf(h_act * (2.0 ** -0.5)))
            f_ref[s * sub:(s + 1) * sub, act_cols] = (gelu2 * h_lin).astype(BF16)
        tail_ref[:, act_cols] = prev_act
        tail_ref[:, lin_cols] = prev_lin

    for s in range(n_sub):
        rows = slice(s * sub, (s + 1) * sub)
        ffn = jnp.dot(f_ref[rows, :], wd_ref[...], preferred_element_type=F32)
        y = alpha * xs[s] + (1.0 + gate2) * ffn
        out_ref[0, rows, :] = _normalize(y) * g2_ref[...] + b2_ref[...]


def _ffn(x, mod, wu, fw, fb, wd, g2, b2, tm, alpha):
    bsz, seq, d = x.shape
    hc = 256
    assert D_FF % hc == 0 and FFN_CONV_WIDTH == 3
    grid = (bsz, seq // tm)
    row_spec = pl.BlockSpec((1, tm, d), lambda b, i: (b, i, 0))
    consts = (wu, fw, fb, wd, g2, b2)
    return pl.pallas_call(
        functools.partial(_ffn_kernel, d=d, alpha=alpha, hc=hc, sub=min(SUB_ROWS, tm)),
        grid=grid,
        in_specs=[row_spec, pl.BlockSpec((1, 1, mod.shape[-1]), lambda b, i: (b, 0, 0))]
                 + [_const_spec(t.shape) for t in consts],
        out_specs=row_spec,
        out_shape=jax.ShapeDtypeStruct((bsz, seq, d), F32),
        scratch_shapes=[pltpu.VMEM((SUBLANES, 2 * D_FF), F32),
                        pltpu.VMEM((tm, D_FF), BF16)],
        compiler_params=pltpu.CompilerParams(
            dimension_semantics=("arbitrary", "arbitrary"), vmem_limit_bytes=VMEM_LIMIT),
        name="conv_ffn",
    )(x, mod, *consts)


def _pack_inproj(w_in, b_in, d):
    g_off = 2 * CONV_CH + 3 * ATTN_W
    pad = LANES - ATTN_HEADS
    w_main = jnp.concatenate(
        [w_in[:, :g_off + ATTN_HEADS], jnp.zeros((d, pad), w_in.dtype)], axis=1).astype(BF16)
    b_main = jnp.concatenate([b_in[:g_off + ATTN_HEADS], jnp.zeros((pad,), b_in.dtype)])[None, :]
    w_gate = w_in[:, g_off + ATTN_HEADS:].astype(BF16)
    b_gate = b_in[g_off + ATTN_HEADS:][None, :]
    return w_main, b_main, w_gate, b_gate


def kernel(x, c, w_ada, b_ada, w_in, b_in, conv_a_w, conv_a_b, ln_conv_g, ln_conv_b,
           w_conv_proj, w_attn_proj, w_mix_out, b_mix_out, ln1_g, ln1_b,
           w_ffn_up, ffn_conv_w, ffn_conv_b, w_ffn_down, ln2_g, ln2_b):
    depth = w_ada.shape[0]
    bsz, seq, d = x.shape
    alpha = (2.0 * depth) ** 0.25
    tm = min(1024, seq)
    tq = min(512, seq)
    mod_all = _modulation(c, w_ada, b_ada)
    row = lambda t: t[None, :]
    for l in range(depth):
        mod = mod_all[l][:, None, :]
        w_main, b_main, w_gate, b_gate = _pack_inproj(w_in[l], b_in[l], d)
        a, q, k, v, ga, gb, aq, ak = _inproj(x, mod, w_main, b_main, w_gate, b_gate, tm)
        o = _attention(q, aq, k, v, ak, tq)
        x = _merge(a, o, ga, gb, x, mod, conv_a_w[l], row(conv_a_b[l]), row(ln_conv_g[l]),
                   row(ln_conv_b[l]), w_conv_proj[l].astype(BF16), w_attn_proj[l].astype(BF16),
                   w_mix_out[l].astype(BF16), row(b_mix_out[l]), row(ln1_g[l]), row(ln1_b[l]),
                   tm, alpha)
        x = _ffn(x, mod, w_ffn_up[l].astype(BF16), ffn_conv_w[l], row(ffn_conv_b[l]),
                 (0.5 * w_ffn_down[l]).astype(BF16), row(ln2_g[l]), row(ln2_b[l]), tm, alpha)
    return x
```

```python
import functools
import math

import jax
import jax.numpy as jnp
from jax import lax
from jax.experimental import pallas as pl
from jax.experimental.pallas import tpu as pltpu

CONV_CH = 512
CONV_WIDTH = 31
ATTN_HEADS = 8
HEAD_DIM = 64
ATTN_W = ATTN_HEADS * HEAD_DIM
D_FF = 2816
FFN_CONV_WIDTH = 3
LN_EPS = 1e-5

LANES = 128
SUBLANES = 8
VMEM_LIMIT = 56 * 1024 * 1024

P_GLU = 0
P_Q = P_GLU + 2 * CONV_CH
P_K = P_Q + ATTN_W
P_V = P_K + ATTN_W
P_F = P_V + ATTN_W

F32 = jnp.float32
BF16 = jnp.bfloat16
NEG_BIG = -1e30
MERGE_SUB_ROWS = 512
FFN_SUB_ROWS = 256
LOG2E = math.log2(math.e)
Q_SCALE = HEAD_DIM ** -0.5 * LOG2E


def _const_spec(shape):
    nd = len(shape)
    return pl.BlockSpec(shape, lambda *_: (0,) * nd, pipeline_mode=pl.Buffered(1))


def _normalize(x):
    mu = jnp.mean(x, axis=-1, keepdims=True)
    xc = x - mu
    var = jnp.mean(xc * xc, axis=-1, keepdims=True)
    return xc * lax.rsqrt(var + LN_EPS)


def _sigmoid(x):
    return jax.nn.sigmoid(x)


def _split3(x):
    hi = x.astype(BF16)
    r1 = x - hi.astype(F32)
    mid = r1.astype(BF16)
    lo = (r1 - mid.astype(F32)).astype(BF16)
    return hi, mid, lo


def _mod_kernel(c_ref, w_ref, b_ref, o_ref):
    c = c_ref[...]
    c_act = c * _sigmoid(c)
    o_ref[0] = jnp.dot(c_act, w_ref[0], preferred_element_type=F32,
                       precision=lax.Precision.HIGHEST) + b_ref[0]


def _modulation(c, w_ada, b_ada):
    depth, d, n = w_ada.shape
    bsz = c.shape[0]
    tn = n // 6
    return pl.pallas_call(
        _mod_kernel,
        grid=(depth, n // tn),
        in_specs=[
            pl.BlockSpec((bsz, d), lambda l, j: (0, 0)),
            pl.BlockSpec((1, d, tn), lambda l, j: (l, 0, j)),
            pl.BlockSpec((1, 1, tn), lambda l, j: (l, 0, j)),
        ],
        out_specs=pl.BlockSpec((1, bsz, tn), lambda l, j: (l, 0, j)),
        out_shape=jax.ShapeDtypeStruct((depth, bsz, n), F32),
        compiler_params=pltpu.CompilerParams(
            dimension_semantics=("arbitrary", "arbitrary"), vmem_limit_bytes=VMEM_LIMIT),
        name="adaln_modulation",
    )(c, w_ada, b_ada.reshape(depth, 1, n))


def _inproj_kernel(x_ref, mod_ref, w_ref, b_ref, wg_ref, bg_ref,
                   a_ref, q_ref, k_ref, v_ref, ga_ref, gb_ref, aq_ref, ak_ref, carry_ref, *, d):
    i = pl.program_id(1)
    tm = x_ref.shape[1]
    shift = mod_ref[0, :, 0:d]
    scale = mod_ref[0, :, d:2 * d]
    u = (_normalize(x_ref[0]) * (1.0 + scale) + shift).astype(BF16)

    def proj(lo, hi):
        return jnp.dot(u, w_ref[:, lo:hi], preferred_element_type=F32) + b_ref[:, lo:hi]

    val = proj(P_GLU, P_GLU + CONV_CH)
    gate = proj(P_GLU + CONV_CH, P_Q)
    a_ref[0] = (val * _sigmoid(gate)).astype(BF16)
    q_ref[0] = (proj(P_Q, P_K) * Q_SCALE).astype(BF16)
    k_ref[0] = proj(P_K, P_V).astype(BF16)
    v_ref[0] = proj(P_V, P_F).astype(BF16)
    ga_ref[0] = _sigmoid(jnp.dot(u, wg_ref[:, 0:d], preferred_element_type=F32)
                         + bg_ref[:, 0:d]).astype(BF16)
    gb_ref[0] = _sigmoid(jnp.dot(u, wg_ref[:, d:2 * d], preferred_element_type=F32)
                         + bg_ref[:, d:2 * d]).astype(BF16)

    zf = proj(P_F, P_F + LANES)
    log_f = jnp.minimum(zf, 0.0) - jnp.log1p(jnp.exp(-jnp.abs(zf)))

    @pl.when(i == 0)
    def _():
        carry_ref[...] = jnp.zeros_like(carry_ref)

    row = lax.broadcasted_iota(jnp.int32, (LANES, LANES), 0)
    col = lax.broadcasted_iota(jnp.int32, (LANES, LANES), 1)
    tri = jnp.where(row >= col, 1.0, 0.0).astype(BF16)
    group = col // ATTN_HEADS
    carry = carry_ref[0:1, :]
    for r in range(tm // LANES):
        rows = slice(r * LANES, (r + 1) * LANES)
        hi, mid, lo = _split3(log_f[rows, :])
        cs = (jnp.dot(tri, hi, preferred_element_type=F32)
              + jnp.dot(tri, mid, preferred_element_type=F32)
              + jnp.dot(tri, lo, preferred_element_type=F32)) + carry
        carry = cs[LANES - 1:LANES, :]
        parts = [p.astype(F32) for p in _split3(cs * LOG2E)]
        ak = jnp.where(group < 6, 1.0, 0.0)
        aq = jnp.where(group < 3, 1.0, 0.0)
        for g, part in enumerate(parts):
            placed_k = part if g == 0 else pltpu.roll(part, g * ATTN_HEADS, 1)
            ak = jnp.where(group == g, -placed_k, ak)
            aq = jnp.where(group == 3 + g, pltpu.roll(part, (3 + g) * ATTN_HEADS, 1), aq)
        ak_ref[0, rows, :] = ak.astype(BF16)
        aq_ref[0, rows, :] = aq.astype(BF16)
    carry_ref[0:1, :] = carry


def _inproj(x, mod, w_main, b_main, w_gate, b_gate, tm):
    bsz, seq, d = x.shape
    n_main = w_main.shape[1]
    grid = (bsz, seq // tm)
    row_spec = lambda w: pl.BlockSpec((1, tm, w), lambda b, i: (b, i, 0))
    out_shape = (
        jax.ShapeDtypeStruct((bsz, seq, CONV_CH), BF16),
        jax.ShapeDtypeStruct((bsz, seq, ATTN_W), BF16),
        jax.ShapeDtypeStruct((bsz, seq, ATTN_W), BF16),
        jax.ShapeDtypeStruct((bsz, seq, ATTN_W), BF16),
        jax.ShapeDtypeStruct((bsz, seq, d), BF16),
        jax.ShapeDtypeStruct((bsz, seq, d), BF16),
        jax.ShapeDtypeStruct((bsz, seq, LANES), BF16),
        jax.ShapeDtypeStruct((bsz, seq, LANES), BF16),
    )
    return pl.pallas_call(
        functools.partial(_inproj_kernel, d=d),
        grid=grid,
        in_specs=[
            row_spec(d),
            pl.BlockSpec((1, 1, mod.shape[-1]), lambda b, i: (b, 0, 0)),
            _const_spec((d, n_main)),
            _const_spec((1, n_main)),
            _const_spec((d, 2 * d)),
            _const_spec((1, 2 * d)),
        ],
        out_specs=(row_spec(CONV_CH), row_spec(ATTN_W), row_spec(ATTN_W), row_spec(ATTN_W),
                   row_spec(d), row_spec(d), row_spec(LANES), row_spec(LANES)),
        out_shape=out_shape,
        scratch_shapes=[pltpu.VMEM((SUBLANES, LANES), F32)],
        compiler_params=pltpu.CompilerParams(
            dimension_semantics=("arbitrary", "arbitrary"), vmem_limit_bytes=VMEM_LIMIT),
        name="inproj",
    )(x, mod, w_main, b_main, w_gate, b_gate)


def _attn_kernel(q_ref, aq_ref, k_ref, v_ref, ak_ref, o_ref, qa_ref, m_ref, acc_ref, *, tq):
    i = pl.program_id(1)
    qb = tq // 2
    lane = lax.broadcasted_iota(jnp.int32, (tq, LANES), 1)
    low_half = lane < HEAD_DIM
    own = (low_half, jnp.logical_not(low_half))
    zero = jnp.zeros((tq, LANES), BF16)

    aq = aq_ref[0]
    for h in range(ATTN_HEADS):
        pair, e = h // 2, h % 2
        q_own = jnp.where(own[e], q_ref[0, :, pair * LANES:(pair + 1) * LANES], zero)
        bias = jnp.where(lane % ATTN_HEADS == h, aq, zero)
        for sub in range(2):
            dst = slice((2 * sub + e) * qb, (2 * sub + e + 1) * qb)
            qa_ref[pair, dst, 0:LANES] = q_own[sub * qb:(sub + 1) * qb, :]
            qa_ref[pair, dst, LANES:2 * LANES] = bias[sub * qb:(sub + 1) * qb, :]

    def keys(rows, pair):
        blk = slice(pair * LANES, (pair + 1) * LANES)
        return jnp.concatenate([k_ref[0, rows, blk], ak_ref[0, rows, :]], axis=1)

    def values(rows, pair, e):
        vv = v_ref[0, rows, pair * LANES:(pair + 1) * LANES]
        low = lax.broadcasted_iota(jnp.int32, vv.shape, 1) < HEAD_DIM
        return jnp.where(low if e == 0 else jnp.logical_not(low), vv, jnp.ones_like(vv))

    def scores(lhs, kk):
        return lax.dot_general(lhs, kk, (((1,), (1,)), ((), ())),
                               preferred_element_type=F32)

    def probs(s, m):
        return jnp.exp2(jnp.concatenate(
            [s[:, c * LANES:(c + 1) * LANES] - m for c in range(s.shape[1] // LANES)],
            axis=1).astype(BF16))

    def row_max(s):
        return jnp.broadcast_to(jnp.max(s, axis=-1, keepdims=True), (s.shape[0], LANES))

    def diagonal_block(q0):
        rows0, rows1 = pl.ds(q0, qb), pl.ds(q0 + qb, qb)
        row = lax.broadcasted_iota(jnp.int32, (qb, qb), 0)
        col = lax.broadcasted_iota(jnp.int32, (qb, qb), 1)
        causal = row >= col
        for pair in range(ATTN_HEADS // 2):
            d0 = scores(qa_ref[pair], keys(rows0, pair))
            d1 = scores(qa_ref[pair, 2 * qb:4 * qb, :], keys(rows1, pair))
            for e in range(2):
                h = 2 * pair + e
                s00 = jnp.where(causal, d0[e * qb:(e + 1) * qb, :], NEG_BIG)
                s10 = d0[(2 + e) * qb:(3 + e) * qb, :]
                s11 = jnp.where(causal, d1[e * qb:(e + 1) * qb, :], NEG_BIG)
                m0 = row_max(s00)
                m1 = jnp.maximum(row_max(s10), row_max(s11))
                p_first = jnp.concatenate([probs(s00, m0), probs(s10, m1)], axis=0)
                pv_first = jnp.dot(p_first, values(rows0, pair, e), preferred_element_type=F32)
                pv_second = jnp.dot(probs(s11, m1), values(rows1, pair, e),
                                    preferred_element_type=F32)
                acc_ref[h, 0:qb, :] = pv_first[0:qb, :]
                acc_ref[h, qb:tq, :] = pv_first[qb:tq, :] + pv_second
                m_ref[h, 0:qb, :] = m0
                m_ref[h, qb:tq, :] = m1

    def full_block(j, carry):
        rows = pl.ds(pl.multiple_of(j * tq, tq), tq)
        for pair in range(ATTN_HEADS // 2):
            s_pair = scores(qa_ref[pair], keys(rows, pair))
            for e in range(2):
                h = 2 * pair + e
                s = jnp.concatenate([s_pair[e * qb:(e + 1) * qb, :],
                                     s_pair[(2 + e) * qb:(3 + e) * qb, :]], axis=0)
                m_prev = m_ref[h]
                m_new = jnp.maximum(m_prev, row_max(s))
                pv = jnp.dot(probs(s, m_new), values(rows, pair, e), preferred_element_type=F32)
                acc_ref[h] = jnp.exp2(m_prev - m_new) * acc_ref[h] + pv
                m_ref[h] = m_new
        return carry

    diagonal_block(pl.multiple_of(i * tq, tq))
    lax.fori_loop(0, i, full_block, 0)

    for pair in range(ATTN_HEADS // 2):
        acc0, acc1 = acc_ref[2 * pair], acc_ref[2 * pair + 1]
        numer = jnp.where(low_half, acc0, acc1)
        denom = pltpu.roll(jnp.where(low_half, acc1, acc0), HEAD_DIM, 1)
        o_ref[0, :, pair * LANES:(pair + 1) * LANES] = (numer / denom).astype(o_ref.dtype)


def _attention(q, aq, k, v, ak, tq):
    bsz, seq, _ = q.shape
    grid = (bsz, seq // tq)
    q_spec = lambda w: pl.BlockSpec((1, tq, w), lambda b, i: (b, i, 0))
    kv_spec = lambda w: pl.BlockSpec((1, seq, w), lambda b, i: (b, 0, 0))
    return pl.pallas_call(
        functools.partial(_attn_kernel, tq=tq),
        grid=grid,
        in_specs=[q_spec(ATTN_W), q_spec(LANES), kv_spec(ATTN_W), kv_spec(ATTN_W), kv_spec(LANES)],
        out_specs=q_spec(ATTN_W),
        out_shape=jax.ShapeDtypeStruct((bsz, seq, ATTN_W), BF16),
        scratch_shapes=[pltpu.VMEM((ATTN_HEADS // 2, 2 * tq, 2 * LANES), BF16),
                        pltpu.VMEM((ATTN_HEADS, tq, LANES), F32),
                        pltpu.VMEM((ATTN_HEADS, tq, LANES), F32)],
        compiler_params=pltpu.CompilerParams(
            dimension_semantics=("arbitrary", "arbitrary"), vmem_limit_bytes=VMEM_LIMIT),
        name="fox_attention",
    )(q, aq, k, v, ak)


def _merge_kernel(a_ref, halo_ref, o_ref, ga_ref, gb_ref, x_ref, mod_ref,
                  cw_ref, cb_ref, lcg_ref, lcb_ref, wc_ref, wa_ref, wm_ref, bm_ref,
                  g1_ref, b1_ref, out_ref, shift_ref, conv_ref, *, d, alpha, halo):
    i = pl.program_id(1)
    tm = a_ref.shape[1]
    ext_rows = halo + tm
    first = halo - (CONV_WIDTH - 1)
    row_block = 128

    for lc in range(CONV_CH // LANES):
        lanes = slice(lc * LANES, (lc + 1) * LANES)
        left = jnp.where(i > 0, halo_ref[0, :, lanes].astype(F32), 0.0)
        ext = jnp.concatenate([left, a_ref[0, :, lanes].astype(F32)], axis=0)
        for r in range(SUBLANES):
            shift_ref[r] = ext if r == 0 else pltpu.roll(ext, ext_rows - r, 0)
        for rb in range(tm // row_block):
            acc = jnp.zeros((row_block, LANES), F32) + cb_ref[:, lanes]
            for tap in range(CONV_WIDTH):
                off = first + tap
                base = (off // SUBLANES) * SUBLANES + rb * row_block
                acc = acc + cw_ref[tap:tap + 1, lanes] * shift_ref[off % SUBLANES,
                                                                   base:base + row_block, :]
            conv_ref[rb * row_block:(rb + 1) * row_block, lanes] = acc
    conv = conv_ref[...]

    gate1 = mod_ref[0, :, 2 * d:3 * d]
    sub = min(MERGE_SUB_ROWS, tm)
    for s in range(tm // sub):
        rows = slice(s * sub, (s + 1) * sub)
        act = _normalize(conv[rows, :]) * lcg_ref[...] + lcb_ref[...]
        act = (act * _sigmoid(act)).astype(BF16)
        y_a = jnp.dot(act, wc_ref[...], preferred_element_type=F32)
        y_b = jnp.dot(o_ref[0, rows, :], wa_ref[...], preferred_element_type=F32)
        merged = (ga_ref[0, rows, :].astype(F32) * y_a
                  + gb_ref[0, rows, :].astype(F32) * y_b).astype(BF16)
        mix = jnp.dot(merged, wm_ref[...], preferred_element_type=F32) + bm_ref[...]
        y = alpha * x_ref[0, rows, :] + (1.0 + gate1) * mix
        out_ref[0, rows, :] = _normalize(y) * g1_ref[...] + b1_ref[...]


def _merge(a, o, ga, gb, x, mod, cw, cb, lcg, lcb, wc, wa, wm, bm, g1, b1, tm, alpha):
    bsz, seq, d = x.shape
    halo = 32
    assert halo >= CONV_WIDTH - 1 and tm % halo == 0
    grid = (bsz, seq // tm)
    row_spec = lambda w: pl.BlockSpec((1, tm, w), lambda b, i: (b, i, 0))
    halo_spec = pl.BlockSpec(
        (1, halo, CONV_CH), lambda b, i: (b, jnp.maximum(i * (tm // halo) - 1, 0), 0))
    consts = (cw, cb, lcg, lcb, wc, wa, wm, bm, g1, b1)
    return pl.pallas_call(
        functools.partial(_merge_kernel, d=d, alpha=alpha, halo=halo),
        grid=grid,
        in_specs=[row_spec(CONV_CH), halo_spec, row_spec(ATTN_W), row_spec(d), row_spec(d),
                  row_spec(d), pl.BlockSpec((1, 1, mod.shape[-1]), lambda b, i: (b, 0, 0))]
                 + [_const_spec(t.shape) for t in consts],
        out_specs=row_spec(d),
        out_shape=jax.ShapeDtypeStruct((bsz, seq, d), F32),
        scratch_shapes=[pltpu.VMEM((SUBLANES, halo + tm, LANES), F32),
                        pltpu.VMEM((tm, CONV_CH), F32)],
        compiler_params=pltpu.CompilerParams(
            dimension_semantics=("arbitrary", "arbitrary"), vmem_limit_bytes=VMEM_LIMIT),
        name="conv_merge",
    )(a, a, o, ga, gb, x, mod, *consts)


def _ffn_kernel(x_ref, mod_ref, wu_ref, fw_ref, fb_ref, wd_ref, g2_ref, b2_ref,
                out_ref, tail_ref, f_ref, *, d, alpha, hc, sub):
    i = pl.program_id(1)
    tm = x_ref.shape[1]
    pad = SUBLANES
    shift = mod_ref[0, :, 3 * d:4 * d]
    scale = mod_ref[0, :, 4 * d:5 * d]
    gate2 = mod_ref[0, :, 5 * d:6 * d]

    @pl.when(i == 0)
    def _():
        tail_ref[...] = jnp.zeros_like(tail_ref)

    def conv_chunk(u, c0, prev):
        cols = slice(c0, c0 + hc)
        h_pre = jnp.dot(u, wu_ref[:, cols], preferred_element_type=F32)
        w0, w1, w2 = (fw_ref[tap:tap + 1, cols] for tap in range(FFN_CONV_WIDTH))
        bias = fb_ref[:, cols]
        body = bias + w2 * h_pre + w1 * pltpu.roll(h_pre, 1, 0) + w0 * pltpu.roll(h_pre, 2, 0)
        top = jnp.concatenate([prev, h_pre[0:pad, :]], axis=0)
        head = (bias + w2 * top[pad:2 * pad, :] + w1 * top[pad - 1:2 * pad - 1, :]
                + w0 * top[pad - 2:2 * pad - 2, :])
        return jnp.concatenate([head, body[pad:, :]], axis=0), h_pre[sub - pad:sub, :]

    n_sub = tm // sub
    xs = [x_ref[0, s * sub:(s + 1) * sub, :] for s in range(n_sub)]
    us = [(_normalize(x) * (1.0 + scale) + shift).astype(BF16) for x in xs]
    for c in range(D_FF // hc):
        act_cols = slice(c * hc, (c + 1) * hc)
        lin_cols = slice(D_FF + c * hc, D_FF + (c + 1) * hc)
        prev_act, prev_lin = tail_ref[:, act_cols], tail_ref[:, lin_cols]
        for s in range(n_sub):
            h_act, prev_act = conv_chunk(us[s], c * hc, prev_act)
            h_lin, prev_lin = conv_chunk(us[s], D_FF + c * hc, prev_lin)
            gelu2 = h_act * (1.0 + lax.erf(h_act * (2.0 ** -0.5)))
            f_ref[s * sub:(s + 1) * sub, act_cols] = (gelu2 * h_lin).astype(BF16)
        tail_ref[:, act_cols] = prev_act
        tail_ref[:, lin_cols] = prev_lin

    for s in range(n_sub):
        rows = slice(s * sub, (s + 1) * sub)
        ffn = jnp.dot(f_ref[rows, :], wd_ref[...], preferred_element_type=F32)
        y = alpha * xs[s] + (1.0 + gate2) * ffn
        out_ref[0, rows, :] = _normalize(y) * g2_ref[...] + b2_ref[...]


def _ffn(x, mod, wu, fw, fb, wd, g2, b2, tm, alpha):
    bsz, seq, d = x.shape
    hc = 256
    assert D_FF % hc == 0 and FFN_CONV_WIDTH == 3
    grid = (bsz, seq // tm)
    row_spec = pl.BlockSpec((1, tm, d), lambda b, i: (b, i, 0))
    consts = (wu, fw, fb, wd, g2, b2)
    return pl.pallas_call(
        functools.partial(_ffn_kernel, d=d, alpha=alpha, hc=hc, sub=min(FFN_SUB_ROWS, tm)),
        grid=grid,
        in_specs=[row_spec, pl.BlockSpec((1, 1, mod.shape[-1]), lambda b, i: (b, 0, 0))]
                 + [_const_spec(t.shape) for t in consts],
        out_specs=row_spec,
        out_shape=jax.ShapeDtypeStruct((bsz, seq, d), F32),
        scratch_shapes=[pltpu.VMEM((SUBLANES, 2 * D_FF), F32),
                        pltpu.VMEM((tm, D_FF), BF16)],
        compiler_params=pltpu.CompilerParams(
            dimension_semantics=("arbitrary", "arbitrary"), vmem_limit_bytes=VMEM_LIMIT),
        name="conv_ffn",
    )(x, mod, *consts)


def _pack_inproj(w_in, b_in, d):
    g_off = 2 * CONV_CH + 3 * ATTN_W
    pad = LANES - ATTN_HEADS
    w_main = jnp.concatenate(
        [w_in[:, :g_off + ATTN_HEADS], jnp.zeros((d, pad), w_in.dtype)], axis=1).astype(BF16)
    b_main = jnp.concatenate([b_in[:g_off + ATTN_HEADS], jnp.zeros((pad,), b_in.dtype)])[None, :]
    w_gate = w_in[:, g_off + ATTN_HEADS:].astype(BF16)
    b_gate = b_in[g_off + ATTN_HEADS:][None, :]
    return w_main, b_main, w_gate, b_gate


def kernel(x, c, w_ada, b_ada, w_in, b_in, conv_a_w, conv_a_b, ln_conv_g, ln_conv_b,
           w_conv_proj, w_attn_proj, w_mix_out, b_mix_out, ln1_g, ln1_b,
           w_ffn_up, ffn_conv_w, ffn_conv_b, w_ffn_down, ln2_g, ln2_b):
    depth = w_ada.shape[0]
    bsz, seq, d = x.shape
    alpha = (2.0 * depth) ** 0.25
    tm = min(1024, seq)
    tq = min(512, seq)
    mod_all = _modulation(c, w_ada, b_ada)
    row = lambda t: t[None, :]
    for l in range(depth):
        mod = mod_all[l][:, None, :]
        w_main, b_main, w_gate, b_gate = _pack_inproj(w_in[l], b_in[l], d)
        a, q, k, v, ga, gb, aq, ak = _inproj(x, mod, w_main, b_main, w_gate, b_gate, tm)
        o = _attention(q, aq, k, v, ak, tq)
        x = _merge(a, o, ga, gb, x, mod, conv_a_w[l], row(conv_a_b[l]), row(ln_conv_g[l]),
                   row(ln_conv_b[l]), w_conv_proj[l].astype(BF16), w_attn_proj[l].astype(BF16),
                   w_mix_out[l].astype(BF16), row(b_mix_out[l]), row(ln1_g[l]), row(ln1_b[l]),
                   tm, alpha)
        x = _ffn(x, mod, w_ffn_up[l].astype(BF16), ffn_conv_w[l], row(ffn_conv_b[l]),
                 (0.5 * w_ffn_down[l]).astype(BF16), row(ln2_g[l]), row(ln2_b[l]), tm, alpha)
    return x
```
